```python
import math
import jax, jax.numpy as jnp
from jax import lax
import numpy as np

D_MODEL = 2048
BATCH = 4
SEQ = 2048
DEPTH = 1
DEC_BATCH = 128
DEC_SEQ = 8
PAST_LEN = 16384
PAGE_SIZE = 128

N_MEM = 256
X_HEADS = 4
X_HEAD_DIM = D_MODEL // 16
D_XATT = X_HEADS * X_HEAD_DIM
D_CONV_A = D_MODEL
CONV_A_W = 3
EXPAND = 2
D_INNER = EXPAND * D_MODEL
SSM_HEAD_DIM = 64
SSM_HEADS = D_INNER // SSM_HEAD_DIM
SSM_GROUPS = 8
D_STATE = 128
CONV_B_W = 4
SSD_CHUNK = 128
D_XBC = D_INNER + 2 * SSM_GROUPS * D_STATE
D_FF = 4 * D_MODEL
EPS = 1e-6
SPLIT_SIZES = (D_CONV_A, D_CONV_A, D_CONV_A, D_INNER, D_XBC, SSM_HEADS, D_XATT, 3 * D_MODEL)
D_IN_PROJ = sum(SPLIT_SIZES)

kernel_name = "hybrid_conv_ssd_memxattn_step"


def _split_points(sizes):
    return [int(v) for v in np.cumsum(np.array(sizes))[:-1]]


def rmsnorm(x, g):
    xf = x.astype(jnp.float32)
    y = xf * lax.rsqrt(jnp.mean(xf * xf, axis=-1, keepdims=True) + EPS)
    return (y * g.astype(jnp.float32)).astype(x.dtype)


def gated_group_rmsnorm(y, z, g):
    b, l, d = y.shape
    u = (y.astype(jnp.float32) * jax.nn.silu(z.astype(jnp.float32))).reshape(b, l, SSM_GROUPS, d // SSM_GROUPS)
    u = u * lax.rsqrt(jnp.mean(u * u, axis=-1, keepdims=True) + EPS)
    return (u.reshape(b, l, d) * g.astype(jnp.float32)).astype(y.dtype)


def causal_dwconv(x, prev, w):
    K = w.shape[0]
    L = x.shape[1]
    u = jnp.concatenate([prev.astype(x.dtype), x], axis=1)
    y = u[:, 0:L] * w[0]
    for k in range(1, K):
        y = y + u[:, k:k + L] * w[k]
    return y, u[:, -(K - 1):]


def ssd(x, dt, A, B, C, h0):
    b, l, nh, p = x.shape
    n = B.shape[-1]
    q = math.gcd(l, SSD_CHUNK)
    c = l // q
    r = nh // SSM_GROUPS
    xr = x.reshape(b, c, q, SSM_GROUPS, r, p)
    dtr = dt.reshape(b, c, q, SSM_GROUPS, r)
    Br = B.reshape(b, c, q, SSM_GROUPS, n)
    Cr = C.reshape(b, c, q, SSM_GROUPS, n)
    acum = jnp.cumsum(dtr * A.reshape(SSM_GROUPS, r), axis=2)
    xdt = xr * dtr[..., None]
    mask = jnp.tril(jnp.ones((q, q), dtype=bool))[None, None, :, :, None, None]
    seg = acum[:, :, :, None] - acum[:, :, None, :]
    decay = jnp.exp(jnp.where(mask, seg, -jnp.inf))
    cb = jnp.einsum('bcqgn,bcsgn->bcqsg', Cr, Br)
    y_diag = jnp.einsum('bcqsgr,bcsgrp->bcqgrp', cb[..., None] * decay, xdt)
    decay_end = jnp.exp(acum[:, :, -1:] - acum)
    states = jnp.einsum('bcsgn,bcsgrp->bcgrpn', Br, xdt * decay_end[..., None])
    chunk_decay = jnp.exp(acum[:, :, -1])

    def step(h, inp):
        s, d = inp
        return h * d[..., None, None] + s, h

    h_final, h_prev = lax.scan(step, h0.reshape(b, SSM_GROUPS, r, p, n),
                               (jnp.moveaxis(states, 1, 0), jnp.moveaxis(chunk_decay, 1, 0)))
    h_prev = jnp.moveaxis(h_prev, 0, 1)
    y_off = jnp.einsum('bcqgn,bcgrpn->bcqgrp', Cr, h_prev) * jnp.exp(acum)[..., None]
    y = (y_diag + y_off).reshape(b, l, nh, p)
    return y, h_final.reshape(b, nh, p, n)


def mem_kv(mem, g, w_kv):
    b, m, _ = mem.shape
    kv = rmsnorm(mem, g) @ w_kv
    k, v = jnp.split(kv, 2, axis=-1)
    return k.reshape(b, m, X_HEADS, X_HEAD_DIM), v.reshape(b, m, X_HEADS, X_HEAD_DIM)


def cross_attn(q, k, v):
    s = jnp.einsum('blhd,bmhd->bhlm', q, k).astype(jnp.float32) * (X_HEAD_DIM ** -0.5)
    pr = jax.nn.softmax(s, axis=-1)
    return jnp.einsum('bhlm,bmhd->blhd', pr.astype(v.dtype), v)


def _layer(x, conv_a_prev, conv_b_prev, ssm_prev, mem_k, mem_v, p):
    b, l, _ = x.shape
    xn = rmsnorm(x, p['norm_mix_pre'])
    proj = xn @ p['w_in']
    b_a, c_a, h_a, z, xbc, dt_raw, q, gates = jnp.split(proj, _split_points(SPLIT_SIZES), axis=-1)
    conv_a_out, conv_a_new = causal_dwconv(c_a * h_a, conv_a_prev, p['conv_a_w'])
    y_a = (b_a * conv_a_out) @ p['w_out_a']
    xbc_c, conv_b_new = causal_dwconv(xbc, conv_b_prev, p['conv_b_w'])
    xbc_c = jax.nn.silu(xbc_c + p['conv_b_bias'])
    xs, Bs, Cs = jnp.split(xbc_c, [D_INNER, D_INNER + SSM_GROUPS * D_STATE], axis=-1)
    xs_h = xs.reshape(b, l, SSM_HEADS, SSM_HEAD_DIM).astype(jnp.float32)
    dt = jax.nn.softplus(dt_raw.astype(jnp.float32) + p['dt_bias'].astype(jnp.float32))
    A = -jnp.exp(p['a_log'].astype(jnp.float32))
    y_s, ssm_new = ssd(xs_h, dt, A,
                       Bs.reshape(b, l, SSM_GROUPS, D_STATE).astype(jnp.float32),
                       Cs.reshape(b, l, SSM_GROUPS, D_STATE).astype(jnp.float32),
                       ssm_prev.astype(jnp.float32))
    y_s = y_s + p['d_skip'].astype(jnp.float32)[:, None] * xs_h
    y_s = y_s.reshape(b, l, D_INNER).astype(x.dtype)
    y_b = gated_group_rmsnorm(y_s, z, p['ssm_norm_w']) @ p['w_out_b']
    o = cross_attn(q.reshape(b, l, X_HEADS, X_HEAD_DIM), mem_k, mem_v).reshape(b, l, D_XATT)
    y_x = o @ p['w_out_x']
    g_a, g_b, g_x = jnp.split(jax.nn.sigmoid(gates), 3, axis=-1)
    mixed = (g_a * y_a + g_b * y_b + g_x * y_x) @ p['w_o']
    x = x + rmsnorm(mixed, p['norm_mix_post'])
    hn = rmsnorm(x, p['norm_mlp_pre'])
    ff = jnp.square(jax.nn.relu(hn @ p['w_ff1'])) @ p['w_ff2']
    x = x + rmsnorm(ff, p['norm_mlp_post'])
    return x, conv_a_new, conv_b_new, ssm_new.astype(ssm_prev.dtype)


def setup_inputs(seed: int = 0) -> dict:
    key = jax.random.key(seed)
    ks = jax.random.split(key, 40)

    def nrm(k, shape, scale):
        return jax.random.normal(k, shape, jnp.float32) * scale

    def gain(k, shape):
        return 1.0 + nrm(k, shape, 0.02)

    dt0 = jnp.exp(jax.random.uniform(ks[30], (DEPTH, SSM_HEADS), jnp.float32, math.log(1e-3), math.log(1e-1)))
    dt_bias = dt0 + jnp.log(-jnp.expm1(-dt0))
    a_log = jnp.log(jax.random.uniform(ks[31], (DEPTH, SSM_HEADS), jnp.float32, 1.0, 16.0))
    return {
        "x_prompt": nrm(ks[0], (BATCH, SEQ, D_MODEL), 1.0),
        "x_sample": nrm(ks[1], (DEC_BATCH, DEC_SEQ, D_MODEL), 1.0),
        "mem_prompt": nrm(ks[2], (BATCH, N_MEM, D_MODEL), 1.0),
        "cache_mem_k": nrm(ks[3], (DEPTH, DEC_BATCH, N_MEM, X_HEADS, X_HEAD_DIM), 1.0),
        "cache_mem_v": nrm(ks[4], (DEPTH, DEC_BATCH, N_MEM, X_HEADS, X_HEAD_DIM), 1.0),
        "state_conv_a": nrm(ks[5], (DEPTH, DEC_BATCH, CONV_A_W - 1, D_CONV_A), 1.0),
        "state_conv_b": nrm(ks[6], (DEPTH, DEC_BATCH, CONV_B_W - 1, D_XBC), 1.0),
        "state_ssm": nrm(ks[7], (DEPTH, DEC_BATCH, SSM_HEADS, SSM_HEAD_DIM, D_STATE), 0.1),
        "norm_mix_pre": gain(ks[8], (DEPTH, D_MODEL)),
        "norm_mix_post": gain(ks[9], (DEPTH, D_MODEL)),
        "norm_mlp_pre": gain(ks[10], (DEPTH, D_MODEL)),
        "norm_mlp_post": gain(ks[11], (DEPTH, D_MODEL)),
        "norm_mem": gain(ks[12], (DEPTH, D_MODEL)),
        "w_in": nrm(ks[13], (DEPTH, D_MODEL, D_IN_PROJ), D_MODEL ** -0.5),
        "conv_a_w": nrm(ks[14], (DEPTH, CONV_A_W, D_CONV_A), CONV_A_W ** -0.5),
        "w_out_a": nrm(ks[15], (DEPTH, D_CONV_A, D_MODEL), D_CONV_A ** -0.5),
        "conv_b_w": nrm(ks[16], (DEPTH, CONV_B_W, D_XBC), CONV_B_W ** -0.5),
        "conv_b_bias": nrm(ks[17], (DEPTH, D_XBC), 0.02),
        "dt_bias": dt_bias,
        "a_log": a_log,
        "d_skip": 1.0 + nrm(ks[18], (DEPTH, SSM_HEADS), 0.1),
        "ssm_norm_w": gain(ks[19], (DEPTH, D_INNER)),
        "w_out_b": nrm(ks[20], (DEPTH, D_INNER, D_MODEL), D_INNER ** -0.5),
        "w_mem_kv": nrm(ks[21], (DEPTH, D_MODEL, 2 * D_XATT), D_MODEL ** -0.5),
        "w_out_x": nrm(ks[22], (DEPTH, D_XATT, D_MODEL), D_XATT ** -0.5),
        "w_o": nrm(ks[23], (DEPTH, D_MODEL, D_MODEL), D_MODEL ** -0.5),
        "w_ff1": nrm(ks[24], (DEPTH, D_MODEL, D_FF), D_MODEL ** -0.5),
        "w_ff2": nrm(ks[25], (DEPTH, D_FF, D_MODEL), D_FF ** -0.5),
    }


def reference(x_prompt, x_sample, mem_prompt, cache_mem_k, cache_mem_v, state_conv_a, state_conv_b,
              state_ssm, norm_mix_pre, norm_mix_post, norm_mlp_pre, norm_mlp_post, norm_mem, w_in,
              conv_a_w, w_out_a, conv_b_w, conv_b_bias, dt_bias, a_log, d_skip, ssm_norm_w, w_out_b,
              w_mem_kv, w_out_x, w_o, w_ff1, w_ff2):
    bp = x_prompt.shape[0]
    yp, ys = x_prompt, x_sample
    mk_p, mv_p, ca_p, cb_p, ss_p, ca_s, cb_s, ss_s = [], [], [], [], [], [], [], []
    for l in range(DEPTH):
        p = dict(norm_mix_pre=norm_mix_pre[l], norm_mix_post=norm_mix_post[l],
                 norm_mlp_pre=norm_mlp_pre[l], norm_mlp_post=norm_mlp_post[l], w_in=w_in[l],
                 conv_a_w=conv_a_w[l], w_out_a=w_out_a[l], conv_b_w=conv_b_w[l],
                 conv_b_bias=conv_b_bias[l], dt_bias=dt_bias[l], a_log=a_log[l], d_skip=d_skip[l],
                 ssm_norm_w=ssm_norm_w[l], w_out_b=w_out_b[l], w_out_x=w_out_x[l], w_o=w_o[l],
                 w_ff1=w_ff1[l], w_ff2=w_ff2[l])
        mkp, mvp = mem_kv(mem_prompt, norm_mem[l], w_mem_kv[l])
        yp, cap, cbp, ssp = _layer(
            yp,
            jnp.zeros((bp, CONV_A_W - 1, D_CONV_A), yp.dtype),
            jnp.zeros((bp, CONV_B_W - 1, D_XBC), yp.dtype),
            jnp.zeros((bp, SSM_HEADS, SSM_HEAD_DIM, D_STATE), state_ssm.dtype),
            mkp, mvp, p)
        ys, cas, cbs, sss = _layer(ys, state_conv_a[l], state_conv_b[l], state_ssm[l],
                                   cache_mem_k[l], cache_mem_v[l], p)
        mk_p.append(mkp); mv_p.append(mvp); ca_p.append(cap); cb_p.append(cbp); ss_p.append(ssp)
        ca_s.append(cas); cb_s.append(cbs); ss_s.append(sss)
    return (yp, ys, jnp.stack(mk_p), jnp.stack(mv_p), jnp.stack(ca_p), jnp.stack(cb_p), jnp.stack(ss_p),
            jnp.stack(ca_s), jnp.stack(cb_s), jnp.stack(ss_s))
```

```python
import functools

import jax
import jax.numpy as jnp
from jax import lax
from jax.experimental import pallas as pl
from jax.experimental.pallas import tpu as pltpu

F32 = jnp.float32
BF16 = jnp.bfloat16

D_MODEL = 2048
BATCH = 4
SEQ = 2048
DEC_BATCH = 128
DEC_SEQ = 8
N_MEM = 256
X_HEADS = 4
X_HEAD_DIM = 128
D_XATT = X_HEADS * X_HEAD_DIM
D_CONV_A = D_MODEL
CONV_A_W = 3
D_INNER = 2 * D_MODEL
SSM_HEAD_DIM = 64
SSM_HEADS = D_INNER // SSM_HEAD_DIM
SSM_GROUPS = 8
HEADS_PER_GROUP = SSM_HEADS // SSM_GROUPS
GROUP_WIDTH = D_INNER // SSM_GROUPS
D_STATE = 128
CONV_B_W = 4
SSD_CHUNK = 128
D_XBC = D_INNER + 2 * SSM_GROUPS * D_STATE
D_FF = 4 * D_MODEL
EPS = 1e-6

T_PROMPT = BATCH * SEQ
T_SAMPLE = DEC_BATCH * DEC_SEQ
T_ALL = T_PROMPT + T_SAMPLE

OFF_BA = 0
OFF_CA = OFF_BA + D_CONV_A
OFF_HA = OFF_CA + D_CONV_A
OFF_Z = OFF_HA + D_CONV_A
OFF_XBC = OFF_Z + D_INNER
OFF_DT = OFF_XBC + D_XBC
OFF_Q = OFF_DT + SSM_HEADS
OFF_GATES = OFF_Q + D_XATT
N_MAIN = OFF_DT
N_QG = D_XATT + 3 * D_MODEL

SUBLANES = 8
LANES = 128
HEAD_LANES = LANES


def _params(semantics, vmem_mb):
    return pltpu.CompilerParams(dimension_semantics=semantics, vmem_limit_bytes=vmem_mb << 20)


def _rms_scale(x, g):
    return x * lax.rsqrt(jnp.mean(x * x, axis=-1, keepdims=True) + EPS) * g


def _softmax_rows(s):
    e = jnp.exp(s - jnp.max(s, axis=-1, keepdims=True))
    return e / jnp.sum(e, axis=-1, keepdims=True)


def _split3(v):
    hi = v.astype(BF16)
    r1 = v - hi.astype(F32)
    mid = r1.astype(BF16)
    lo = (r1 - mid.astype(F32)).astype(BF16)
    return jnp.concatenate([hi, mid, lo], axis=1)


def _norm_cast2_kernel(xp_ref, xs_ref, g_ref, o_ref, *, n_prompt_blocks):
    i = pl.program_id(0)

    @pl.when(i < n_prompt_blocks)
    def _():
        o_ref[...] = _rms_scale(xp_ref[...], g_ref[...]).astype(o_ref.dtype)

    @pl.when(i >= n_prompt_blocks)
    def _():
        o_ref[...] = _rms_scale(xs_ref[...], g_ref[...]).astype(o_ref.dtype)


def _norm_cast2(xp, xs, g, *, tm=512):
    n_p, n_s = xp.shape[0] // tm, xs.shape[0] // tm
    d = xp.shape[1]
    return pl.pallas_call(
        functools.partial(_norm_cast2_kernel, n_prompt_blocks=n_p),
        grid=(n_p + n_s,),
        in_specs=[
            pl.BlockSpec((tm, d), lambda i: (jnp.minimum(i, n_p - 1), 0)),
            pl.BlockSpec((tm, d), lambda i: (jnp.maximum(i - n_p, 0), 0)),
            pl.BlockSpec((1, d), lambda i: (0, 0)),
        ],
        out_specs=pl.BlockSpec((tm, d), lambda i: (i, 0)),
        out_shape=jax.ShapeDtypeStruct((xp.shape[0] + xs.shape[0], d), BF16),
        compiler_params=_params(("arbitrary",), 32),
        name="norm_cast2",
    )(xp, xs, g)


def _norm_cast_kernel(x_ref, g_ref, o_ref):
    o_ref[...] = _rms_scale(x_ref[...], g_ref[...]).astype(o_ref.dtype)


def _norm_cast(x, g, *, tm=512):
    m, d = x.shape
    return pl.pallas_call(
        _norm_cast_kernel,
        grid=(m // tm,),
        in_specs=[pl.BlockSpec((tm, d), lambda i: (i, 0)), pl.BlockSpec((1, d), lambda i: (0, 0))],
        out_specs=pl.BlockSpec((tm, d), lambda i: (i, 0)),
        out_shape=jax.ShapeDtypeStruct((m, d), BF16),
        compiler_params=_params(("arbitrary",), 32),
        name="norm_cast",
    )(x, g)


def _mm_kernel(a_ref, w_ref, o_ref, wb_ref, *, epilogue):
    @pl.when(pl.program_id(1) == 0)
    def _():
        wb_ref[...] = w_ref[...].astype(BF16)

    acc = jnp.dot(a_ref[...], wb_ref[...], preferred_element_type=F32)
    if epilogue == "relu2":
        acc = jnp.square(jnp.maximum(acc, 0.0))
    o_ref[...] = acc.astype(o_ref.dtype)


def _matmul(a, w, *, n_out, tm, tn, vmem_mb, out_dtype=F32, epilogue=None,
            single_buffer_w=False, name="matmul"):
    m, k = a.shape
    w_kwargs = {"pipeline_mode": pl.Buffered(1)} if single_buffer_w else {}
    return pl.pallas_call(
        functools.partial(_mm_kernel, epilogue=epilogue),
        grid=(n_out // tn, m // tm),
        in_specs=[
            pl.BlockSpec((tm, k), lambda n, i: (i, 0)),
            pl.BlockSpec((k, tn), lambda n, i: (0, n), **w_kwargs),
        ],
        out_specs=pl.BlockSpec((tm, tn), lambda n, i: (i, n)),
        out_shape=jax.ShapeDtypeStruct((m, n_out), out_dtype),
        scratch_shapes=[pltpu.VMEM((k, tn), BF16)],
        compiler_params=_params(("arbitrary", "arbitrary"), vmem_mb),
        name=name,
    )(a, w)


def _dt_kernel(a_ref, w_ref, b_ref, o_ref):
    raw = jnp.dot(a_ref[...], w_ref[...].astype(BF16), preferred_element_type=F32) + b_ref[...]
    o_ref[...] = jnp.maximum(raw, 0.0) + jnp.log1p(jnp.exp(-jnp.abs(raw)))


def _dt_proj(xn, w_dt, dt_bias, *, tm=1024):
    m, k = xn.shape
    return pl.pallas_call(
        _dt_kernel,
        grid=(m // tm,),
        in_specs=[
            pl.BlockSpec((tm, k), lambda i: (i, 0)),
            pl.BlockSpec((k, HEAD_LANES), lambda i: (0, 0)),
            pl.BlockSpec((1, HEAD_LANES), lambda i: (0, 0)),
        ],
        out_specs=pl.BlockSpec((tm, HEAD_LANES), lambda i: (i, 0)),
        out_shape=jax.ShapeDtypeStruct((m, HEAD_LANES), F32),
        compiler_params=_params(("arbitrary",), 32),
        name="dt_proj",
    )(xn, w_dt, dt_bias)


def _causal_taps_2d(u, w, taps):
    t = lax.broadcasted_iota(jnp.int32, u.shape, 0)
    acc = None
    for j in range(taps):
        d = taps - 1 - j
        term = u if d == 0 else jnp.where(t >= d, pltpu.roll(u, d, axis=0), 0.0)
        term = term * w[j:j + 1, :]
        acc = term if acc is None else acc + term
    return acc


def _causal_taps_3d(u, hist, w, taps):
    c = u.shape[-1]
    t = lax.broadcasted_iota(jnp.int32, u.shape, 1)
    acc = None
    for j in range(taps):
        d = taps - 1 - j
        if d == 0:
            term = u
        else:
            term = jnp.where(t >= d, pltpu.roll(u, d, axis=1), pltpu.roll(hist, d, axis=1))
        term = term * w[j:j + 1, :].reshape(1, 1, c)
        acc = term if acc is None else acc + term
    return acc


def _conv_a_prompt_kernel(b_ref, c_ref, h_ref, w_ref, u_ref, st_ref):
    ch = c_ref[...] * h_ref[...]
    conv = _causal_taps_2d(ch, w_ref[...], CONV_A_W)
    u_ref[...] = (b_ref[...] * conv).astype(u_ref.dtype)
    n = ch.shape[0]
    k = CONV_A_W - 1
    st_ref[0] = c_ref[n - k:n, :] * h_ref[n - k:n, :]


def _conv_a_sample_kernel(b_ref, c_ref, h_ref, hist_ref, w_ref, alias_ref, u_ref, st_ref):
    del alias_ref
    s, _, c = hist_ref.shape
    ch = (c_ref[...] * h_ref[...]).reshape(s, DEC_SEQ, c)
    conv = _causal_taps_3d(ch, hist_ref[...], w_ref[...], CONV_A_W)
    u = b_ref[...].reshape(s, DEC_SEQ, c) * conv
    u_ref[...] = u.reshape(s * DEC_SEQ, c).astype(u_ref.dtype)
    st_ref[...] = ch[:, DEC_SEQ - (CONV_A_W - 1):, :]


def _conv_a(proj, conv_w, hist_s, *, tc=256):
    nb = D_CONV_A // tc
    k = CONV_A_W - 1
    u, st_p = pl.pallas_call(
        _conv_a_prompt_kernel,
        grid=(BATCH, nb),
        in_specs=[
            pl.BlockSpec((SEQ, tc), lambda b, j: (b, OFF_BA // tc + j)),
            pl.BlockSpec((SEQ, tc), lambda b, j: (b, OFF_CA // tc + j)),
            pl.BlockSpec((SEQ, tc), lambda b, j: (b, OFF_HA // tc + j)),
            pl.BlockSpec((CONV_A_W, tc), lambda b, j: (0, j)),
        ],
        out_specs=[
            pl.BlockSpec((SEQ, tc), lambda b, j: (b, j)),
            pl.BlockSpec((1, k, tc), lambda b, j: (b, 0, j)),
        ],
        out_shape=[
            jax.ShapeDtypeStruct((T_ALL, D_CONV_A), BF16),
            jax.ShapeDtypeStruct((BATCH, k, D_CONV_A), F32),
        ],
        compiler_params=_params(("arbitrary", "arbitrary"), 40),
        name="conv_a_prompt",
    )(proj, proj, proj, conv_w)
    rb = T_PROMPT // T_SAMPLE
    u, st_s = pl.pallas_call(
        _conv_a_sample_kernel,
        grid=(nb,),
        in_specs=[
            pl.BlockSpec((T_SAMPLE, tc), lambda j: (rb, OFF_BA // tc + j)),
            pl.BlockSpec((T_SAMPLE, tc), lambda j: (rb, OFF_CA // tc + j)),
            pl.BlockSpec((T_SAMPLE, tc), lambda j: (rb, OFF_HA // tc + j)),
            pl.BlockSpec((DEC_BATCH, DEC_SEQ, tc), lambda j: (0, 0, j)),
            pl.BlockSpec((CONV_A_W, tc), lambda j: (0, j)),
            pl.BlockSpec(memory_space=pl.ANY),
        ],
        out_specs=[
            pl.BlockSpec((T_SAMPLE, tc), lambda j: (rb, j)),
            pl.BlockSpec((DEC_BATCH, k, tc), lambda j: (0, 0, j)),
        ],
        out_shape=[
            jax.ShapeDtypeStruct((T_ALL, D_CONV_A), BF16),
            jax.ShapeDtypeStruct((DEC_BATCH, k, D_CONV_A), F32),
        ],
        input_output_aliases={5: 0},
        compiler_params=_params(("arbitrary",), 40),
        name="conv_a_sample",
    )(proj, proj, proj, hist_s, conv_w, u)
    return u, st_p, st_s


def _conv_b_prompt_kernel(x_ref, w_ref, bias_ref, o_ref, st_ref):
    x = x_ref[...]
    conv = _causal_taps_2d(x, w_ref[...], CONV_B_W) + bias_ref[...]
    o_ref[...] = conv * jax.nn.sigmoid(conv)
    n = x.shape[0]
    k = CONV_B_W - 1
    st_ref[0] = x_ref[n - k:n, :]


def _conv_b_sample_kernel(x_ref, hist_ref, w_ref, bias_ref, alias_ref, o_ref, st_ref):
    del alias_ref
    s, _, c = hist_ref.shape
    x = x_ref[...].reshape(s, DEC_SEQ, c)
    conv = _causal_taps_3d(x, hist_ref[...], w_ref[...], CONV_B_W) + bias_ref[...].reshape(1, 1, c)
    o_ref[...] = (conv * jax.nn.sigmoid(conv)).reshape(s * DEC_SEQ, c)
    st_ref[...] = x[:, DEC_SEQ - (CONV_B_W - 1):, :]


def _conv_b(proj, conv_w, conv_bias, hist_s, *, tc=256):
    nb = D_XBC // tc
    k = CONV_B_W - 1
    y, st_p = pl.pallas_call(
        _conv_b_prompt_kernel,
        grid=(BATCH, nb),
        in_specs=[
            pl.BlockSpec((SEQ, tc), lambda b, j: (b, OFF_XBC // tc + j)),
            pl.BlockSpec((CONV_B_W, tc), lambda b, j: (0, j)),
            pl.BlockSpec((1, tc), lambda b, j: (0, j)),
        ],
        out_specs=[
            pl.BlockSpec((SEQ, tc), lambda b, j: (b, j)),
            pl.BlockSpec((1, k, tc), lambda b, j: (b, 0, j)),
        ],
        out_shape=[
            jax.ShapeDtypeStruct((T_ALL, D_XBC), F32),
            jax.ShapeDtypeStruct((BATCH, k, D_XBC), F32),
        ],
        compiler_params=_params(("arbitrary", "arbitrary"), 40),
        name="conv_b_prompt",
    )(proj, conv_w, conv_bias)
    rb = T_PROMPT // T_SAMPLE
    y, st_s = pl.pallas_call(
        _conv_b_sample_kernel,
        grid=(nb,),
        in_specs=[
            pl.BlockSpec((T_SAMPLE, tc), lambda j: (rb, OFF_XBC // tc + j)),
            pl.BlockSpec((DEC_BATCH, DEC_SEQ, tc), lambda j: (0, 0, j)),
            pl.BlockSpec((CONV_B_W, tc), lambda j: (0, j)),
            pl.BlockSpec((1, tc), lambda j: (0, j)),
            pl.BlockSpec(memory_space=pl.ANY),
        ],
        out_specs=[
            pl.BlockSpec((T_SAMPLE, tc), lambda j: (rb, j)),
            pl.BlockSpec((DEC_BATCH, k, tc), lambda j: (0, 0, j)),
        ],
        out_shape=[
            jax.ShapeDtypeStruct((T_ALL, D_XBC), F32),
            jax.ShapeDtypeStruct((DEC_BATCH, k, D_XBC), F32),
        ],
        input_output_aliases={4: 0},
        compiler_params=_params(("arbitrary",), 40),
        name="conv_b_sample",
    )(proj, hist_s, conv_w, conv_bias, y)
    return y, st_p, st_s


def _gated_group_norm(y, z, w):
    u = y * (z * jax.nn.sigmoid(z))
    return u * lax.rsqrt(jnp.mean(u * u, axis=-1, keepdims=True) + EPS) * w


def _ssd_prompt_kernel(xs_ref, b_ref, c_ref, dt_ref, zlo_ref, zhi_ref, alog_ref, dskip_ref,
                       nw_ref, e3_ref, nb_ref, st_ref, state_ref, *, n_chunks):
    q = SSD_CHUNK
    ci = pl.program_id(1)

    @pl.when(ci == 0)
    def _():
        state_ref[...] = jnp.zeros_like(state_ref)

    row = lax.broadcasted_iota(jnp.int32, (q, q), 0)
    col = lax.broadcasted_iota(jnp.int32, (q, q), 1)
    causal = row >= col
    dt = dt_ref[...]
    da = dt * (-jnp.exp(alog_ref[...]))
    acum = jnp.dot(causal.astype(F32), da, preferred_element_type=F32,
                   precision=lax.Precision.HIGHEST)
    acum_t = acum.T
    a_last = acum[q - 1:q, :]
    per_head = jnp.concatenate(
        [dt, jnp.exp(a_last - acum), jnp.exp(acum), jnp.broadcast_to(jnp.exp(a_last), (SUBLANES, HEAD_LANES))],
        axis=0)
    wide = jnp.dot(_split3(per_head), e3_ref[...], preferred_element_type=F32)
    dt_e, dend_e, eac_e, cd_e = wide[0:q], wide[q:2 * q], wide[2 * q:3 * q], wide[3 * q:3 * q + 1]

    lane = lax.broadcasted_iota(jnp.int32, (q, LANES), 1)
    first_head = lane < SSM_HEAD_DIM
    for g in range(SSM_GROUPS):
        gs = slice(g * GROUP_WIDTH, (g + 1) * GROUP_WIDTH)
        ns = slice(g * D_STATE, (g + 1) * D_STATE)
        xg = xs_ref[:, gs]
        xdt = xg * dt_e[:, gs]
        bg = b_ref[:, ns].astype(BF16)
        cg = c_ref[:, ns].astype(BF16)
        cb = lax.dot_general(cg, bg, (((1,), (1,)), ((), ())), preferred_element_type=F32)
        s_prev = state_ref[g]
        y_off = jnp.dot(cg, s_prev.astype(BF16), preferred_element_type=F32)
        xw = (xdt * dend_e[:, gs]).astype(BF16)
        s_add = lax.dot_general(bg, xw, (((0,), (0,)), ((), ())), preferred_element_type=F32)
        state_ref[g] = s_prev * cd_e[:, gs] + s_add
        pairs = []
        for k in range(HEADS_PER_GROUP // 2):
            h0 = g * HEADS_PER_GROUP + 2 * k
            ms = []
            for h in (h0, h0 + 1):
                seg = acum[:, h:h + 1] - acum_t[h:h + 1, :]
                decay = jnp.exp(jnp.where(causal, seg, -jnp.inf))
                ms.append((cb * decay).astype(BF16))
            lhs = jnp.concatenate(ms, axis=1)
            xp = xdt[:, k * LANES:(k + 1) * LANES]
            rhs = jnp.concatenate([jnp.where(first_head, xp, 0.0).astype(BF16),
                                   jnp.where(first_head, 0.0, xp).astype(BF16)], axis=0)
            pairs.append(jnp.dot(lhs, rhs, preferred_element_type=F32))
        y_diag = jnp.concatenate(pairs, axis=1)
        y = (y_diag + y_off * eac_e[:, gs]) + dskip_ref[:, gs] * xg
        z_ref = zlo_ref if g < SSM_GROUPS // 2 else zhi_ref
        zs = slice((g % (SSM_GROUPS // 2)) * GROUP_WIDTH, (g % (SSM_GROUPS // 2) + 1) * GROUP_WIDTH)
        nb_ref[:, gs] = _gated_group_norm(y, z_ref[:, zs], nw_ref[:, gs]).astype(nb_ref.dtype)

    @pl.when(ci == n_chunks - 1)
    def _():
        for g in range(SSM_GROUPS):
            st_ref[0, g * HEADS_PER_GROUP:(g + 1) * HEADS_PER_GROUP] = (
                state_ref[g].T.reshape(HEADS_PER_GROUP, SSM_HEAD_DIM, D_STATE))


def _ssd_prompt(xbc, dt, proj, a_log, d_skip_e, norm_w, e3):
    q = SSD_CHUNK
    n_chunks = SEQ // q
    zb = OFF_Z // (D_INNER // 2)
    row = lambda b, c: b * n_chunks + c
    return pl.pallas_call(
        functools.partial(_ssd_prompt_kernel, n_chunks=n_chunks),
        grid=(BATCH, n_chunks),
        in_specs=[
            pl.BlockSpec((q, D_INNER), lambda b, c: (row(b, c), 0)),
            pl.BlockSpec((q, SSM_GROUPS * D_STATE), lambda b, c: (row(b, c), D_INNER // (SSM_GROUPS * D_STATE))),
            pl.BlockSpec((q, SSM_GROUPS * D_STATE), lambda b, c: (row(b, c), D_INNER // (SSM_GROUPS * D_STATE) + 1)),
            pl.BlockSpec((q, HEAD_LANES), lambda b, c: (row(b, c), 0)),
            pl.BlockSpec((q, D_INNER // 2), lambda b, c: (row(b, c), zb)),
            pl.BlockSpec((q, D_INNER // 2), lambda b, c: (row(b, c), zb + 1)),
            pl.BlockSpec((1, HEAD_LANES), lambda b, c: (0, 0)),
            pl.BlockSpec((1, D_INNER), lambda b, c: (0, 0)),
            pl.BlockSpec((1, D_INNER), lambda b, c: (0, 0)),
            pl.BlockSpec((3 * HEAD_LANES, D_INNER), lambda b, c: (0, 0)),
        ],
        out_specs=[
            pl.BlockSpec((q, D_INNER), lambda b, c: (row(b, c), 0)),
            pl.BlockSpec((1, SSM_HEADS, SSM_HEAD_DIM, D_STATE), lambda b, c: (b, 0, 0, 0)),
        ],
        out_shape=[
            jax.ShapeDtypeStruct((T_ALL, D_INNER), BF16),
            jax.ShapeDtypeStruct((BATCH, SSM_HEADS, SSM_HEAD_DIM, D_STATE), F32),
        ],
        scratch_shapes=[pltpu.VMEM((SSM_GROUPS, D_STATE, GROUP_WIDTH), F32)],
        compiler_params=_params(("arbitrary", "arbitrary"), 48),
        name="ssd_prompt",
    )(xbc, xbc, xbc, dt, proj, proj, a_log, d_skip_e, norm_w, e3)


def _ssd_sample_kernel(xs_ref, b_ref, c_ref, dt_ref, zlo_ref, zhi_ref, h0_ref, alog_ref, dskip_ref,
                       nw_ref, e3_ref, gsum_ref, alias_ref, nb_ref, hout_ref, y_ref, *, seqs):
    del alias_ref
    q = DEC_SEQ
    tri = (lax.broadcasted_iota(jnp.int32, (q, q), 0) >= lax.broadcasted_iota(jnp.int32, (q, q), 1)).astype(F32)
    qi = lax.broadcasted_iota(jnp.int32, (q, HEAD_LANES), 0)
    neg_a = -jnp.exp(alog_ref[...])
    for j in range(seqs):
        rows = slice(j * q, (j + 1) * q)
        dt = dt_ref[rows, :]
        acum = jnp.dot(tri, dt * neg_a, preferred_element_type=F32, precision=lax.Precision.HIGHEST)
        a_last = acum[q - 1:q, :]
        cd = jnp.exp(a_last)
        cb16 = c_ref[rows, :].astype(BF16)
        bb16 = b_ref[rows, :].astype(BF16)
        cf = cb16.astype(F32)
        bf = bb16.astype(F32)
        decays, prods = [], []
        for s in range(q):
            seg = acum - acum[s:s + 1, :]
            decays.append(jnp.exp(jnp.where(qi >= s, seg, -jnp.inf)))
            prods.append(cf * bf[s:s + 1, :])
        decay = jnp.concatenate(decays, axis=0)
        prod = jnp.concatenate(prods, axis=0)
        p_hi = prod.astype(BF16)
        p_lo = (prod - p_hi.astype(F32)).astype(BF16)
        cbx = jnp.dot(jnp.concatenate([p_hi, p_lo], axis=1), gsum_ref[...], preferred_element_type=F32)
        per_head = jnp.concatenate(
            [cbx * decay, dt, jnp.exp(a_last - acum), jnp.exp(acum), jnp.broadcast_to(cd, (SUBLANES, HEAD_LANES))],
            axis=0)
        wide = jnp.dot(_split3(per_head), e3_ref[...], preferred_element_type=F32)
        n0 = q * q
        w_e = wide[0:n0]
        dt_e, dend_e, eac_e = wide[n0:n0 + q], wide[n0 + q:n0 + 2 * q], wide[n0 + 2 * q:n0 + 3 * q]
        x = xs_ref[rows, :]
        xdt = x * dt_e
        y_diag = None
        for s in range(q):
            term = w_e[s * q:(s + 1) * q] * xdt[s:s + 1, :]
            y_diag = term if y_diag is None else y_diag + term
        xw = (xdt * dend_e).astype(BF16)
        for g in range(SSM_GROUPS):
            gs = slice(g * GROUP_WIDTH, (g + 1) * GROUP_WIDTH)
            ns = slice(g * D_STATE, (g + 1) * D_STATE)
            hs = slice(g * HEADS_PER_GROUP, (g + 1) * HEADS_PER_GROUP)
            h_prev = h0_ref[j, hs].reshape(GROUP_WIDTH, D_STATE)
            y_off = lax.dot_general(cb16[:, ns], h_prev.astype(BF16), (((1,), (1,)), ((), ())),
                                    preferred_element_type=F32)
            s_add = lax.dot_general(xw[:, gs], bb16[:, ns], (((0,), (0,)), ((), ())),
                                    preferred_element_type=F32)
            for r in range(HEADS_PER_GROUP):
                h = g * HEADS_PER_GROUP + r
                hout_ref[j, h] = (h0_ref[j, h] * cd[:, h:h + 1]
                                  + s_add[r * SSM_HEAD_DIM:(r + 1) * SSM_HEAD_DIM, :])
            y = (y_diag[:, gs] + y_off * eac_e[:, gs]) + dskip_ref[:, gs] * x[:, gs]
            z_ref = zlo_ref if g < SSM_GROUPS // 2 else zhi_ref
            zs = slice((g % (SSM_GROUPS // 2)) * GROUP_WIDTH, (g % (SSM_GROUPS // 2) + 1) * GROUP_WIDTH)
            y_ref[rows, gs] = _gated_group_norm(y, z_ref[rows, zs], nw_ref[:, gs])
    nb_ref[...] = y_ref[...].astype(nb_ref.dtype)


def _ssd_sample(xbc, dt, proj, h0, a_log, d_skip_e, norm_w, e3, gsum, nb, *, seqs=2):
    rows = seqs * DEC_SEQ
    r0 = T_PROMPT // rows
    zb = OFF_Z // (D_INNER // 2)
    bc = SSM_GROUPS * D_STATE
    return pl.pallas_call(
        functools.partial(_ssd_sample_kernel, seqs=seqs),
        grid=(DEC_BATCH // seqs,),
        in_specs=[
            pl.BlockSpec((rows, D_INNER), lambda i: (r0 + i, 0)),
            pl.BlockSpec((rows, bc), lambda i: (r0 + i, D_INNER // bc)),
            pl.BlockSpec((rows, bc), lambda i: (r0 + i, D_INNER // bc + 1)),
            pl.BlockSpec((rows, HEAD_LANES), lambda i: (r0 + i, 0)),
            pl.BlockSpec((rows, D_INNER // 2), lambda i: (r0 + i, zb)),
            pl.BlockSpec((rows, D_INNER // 2), lambda i: (r0 + i, zb + 1)),
            pl.BlockSpec((seqs, SSM_HEADS, SSM_HEAD_DIM, D_STATE), lambda i: (i, 0, 0, 0)),
            pl.BlockSpec((1, HEAD_LANES), lambda i: (0, 0)),
            pl.BlockSpec((1, D_INNER), lambda i: (0, 0)),
            pl.BlockSpec((1, D_INNER), lambda i: (0, 0)),
            pl.BlockSpec((3 * HEAD_LANES, D_INNER), lambda i: (0, 0)),
            pl.BlockSpec((2 * bc, HEAD_LANES), lambda i: (0, 0)),
            pl.BlockSpec(memory_space=pl.ANY),
        ],
        out_specs=[
            pl.BlockSpec((rows, D_INNER), lambda i: (r0 + i, 0)),
            pl.BlockSpec((seqs, SSM_HEADS, SSM_HEAD_DIM, D_STATE), lambda i: (i, 0, 0, 0)),
        ],
        out_shape=[
            jax.ShapeDtypeStruct((T_ALL, D_INNER), BF16),
            jax.ShapeDtypeStruct((DEC_BATCH, SSM_HEADS, SSM_HEAD_DIM, D_STATE), F32),
        ],
        scratch_shapes=[pltpu.VMEM((rows, D_INNER), F32)],
        input_output_aliases={12: 0},
        compiler_params=_params(("arbitrary",), 48),
        name="ssd_sample",
    )(xbc, xbc, xbc, dt, proj, proj, h0, a_log, d_skip_e, norm_w, e3, gsum, nb)


def _attn_body(q_ref, k_ref, v_ref, o_ref, *, seqs, tq):
    scale = X_HEAD_DIM ** -0.5
    outs = []
    for j in range(seqs):
        q = q_ref[j * tq:(j + 1) * tq, :]
        k = k_ref[j]
        v = v_ref[j]
        heads = []
        for h in range(X_HEADS):
            hs = slice(h * X_HEAD_DIM, (h + 1) * X_HEAD_DIM)
            s = lax.dot_general(q[:, hs].astype(BF16), k[:, hs].astype(BF16), (((1,), (1,)), ((), ())),
                                preferred_element_type=F32) * scale
            p = _softmax_rows(s)
            heads.append(jnp.dot(p.astype(BF16), v[:, hs].astype(BF16), preferred_element_type=F32))
        outs.append(jnp.concatenate(heads, axis=1))
    o = outs[0] if seqs == 1 else jnp.concatenate(outs, axis=0)
    o_ref[...] = o.astype(o_ref.dtype)


def _attn_prompt_kernel(q_ref, k_ref, v_ref, o_ref, *, seqs, tq):
    _attn_body(q_ref, k_ref, v_ref, o_ref, seqs=seqs, tq=tq)


def _attn_sample_kernel(q_ref, k_ref, v_ref, alias_ref, o_ref, *, seqs, tq):
    del alias_ref
    _attn_body(q_ref, k_ref, v_ref, o_ref, seqs=seqs, tq=tq)


def _attention(qg, k_p, v_p, k_s, v_s, *, tq=512, seqs=8):
    nq = SEQ // tq
    o = pl.pallas_call(
        functools.partial(_attn_prompt_kernel, seqs=1, tq=tq),
        grid=(BATCH, nq),
        in_specs=[
            pl.BlockSpec((tq, D_XATT), lambda b, i: (b * nq + i, 0)),
            pl.BlockSpec((1, N_MEM, D_XATT), lambda b, i: (b, 0, 0)),
            pl.BlockSpec((1, N_MEM, D_XATT), lambda b, i: (b, 0, 0)),
        ],
        out_specs=pl.BlockSpec((tq, D_XATT), lambda b, i: (b * nq + i, 0)),
        out_shape=jax.ShapeDtypeStruct((T_ALL, D_XATT), BF16),
        compiler_params=_params(("arbitrary", "arbitrary"), 32),
        name="attn_prompt",
    )(qg, k_p, v_p)
    rows = seqs * DEC_SEQ
    r0 = T_PROMPT // rows
    return pl.pallas_call(
        functools.partial(_attn_sample_kernel, seqs=seqs, tq=DEC_SEQ),
        grid=(DEC_BATCH // seqs,),
        in_specs=[
            pl.BlockSpec((rows, D_XATT), lambda i: (r0 + i, 0)),
            pl.BlockSpec((seqs, N_MEM, D_XATT), lambda i: (i, 0, 0)),
            pl.BlockSpec((seqs, N_MEM, D_XATT), lambda i: (i, 0, 0)),
            pl.BlockSpec(memory_space=pl.ANY),
        ],
        out_specs=pl.BlockSpec((rows, D_XATT), lambda i: (r0 + i, 0)),
        out_shape=jax.ShapeDtypeStruct((T_ALL, D_XATT), BF16),
        input_output_aliases={3: 0},
        compiler_params=_params(("arbitrary",), 40),
        name="attn_sample",
    )(qg, k_s, v_s, o)


def _merge_kernel(ua_ref, nb_ref, ox_ref, ga_ref, gb_ref, gx_ref, wa_ref, wb_ref, wx_ref, o_ref,
                  wa16_ref, wb16_ref, wx16_ref):
    @pl.when(pl.program_id(1) == 0)
    def _():
        wa16_ref[...] = wa_ref[...].astype(BF16)
        wb16_ref[...] = wb_ref[...].astype(BF16)
        wx16_ref[...] = wx_ref[...].astype(BF16)

    ya = jnp.dot(ua_ref[...], wa16_ref[...], preferred_element_type=F32)
    yb = jnp.dot(nb_ref[...], wb16_ref[...], preferred_element_type=F32)
    yx = jnp.dot(ox_ref[...], wx16_ref[...], preferred_element_type=F32)
    merged = (jax.nn.sigmoid(ga_ref[...]) * ya + jax.nn.sigmoid(gb_ref[...]) * yb) + jax.nn.sigmoid(gx_ref[...]) * yx
    o_ref[...] = merged.astype(o_ref.dtype)


def _merge(u_a, nb, ox, qg, w_a, w_b, w_x, *, tm=512, tn=512):
    g0 = D_XATT // tn
    gstep = D_MODEL // tn
    once = {"pipeline_mode": pl.Buffered(1)}
    return pl.pallas_call(
        _merge_kernel,
        grid=(D_MODEL // tn, T_ALL // tm),
        in_specs=[
            pl.BlockSpec((tm, D_CONV_A), lambda n, i: (i, 0)),
            pl.BlockSpec((tm, D_INNER), lambda n, i: (i, 0)),
            pl.BlockSpec((tm, D_XATT), lambda n, i: (i, 0)),
            pl.BlockSpec((tm, tn), lambda n, i: (i, g0 + n)),
            pl.BlockSpec((tm, tn), lambda n, i: (i, g0 + gstep + n)),
            pl.BlockSpec((tm, tn), lambda n, i: (i, g0 + 2 * gstep + n)),
            pl.BlockSpec((D_CONV_A, tn), lambda n, i: (0, n), **once),
            pl.BlockSpec((D_INNER, tn), lambda n, i: (0, n), **once),
            pl.BlockSpec((D_XATT, tn), lambda n, i: (0, n), **once),
        ],
        out_specs=pl.BlockSpec((tm, tn), lambda n, i: (i, n)),
        out_shape=jax.ShapeDtypeStruct((T_ALL, D_MODEL), BF16),
        scratch_shapes=[pltpu.VMEM((D_CONV_A, tn), BF16), pltpu.VMEM((D_INNER, tn), BF16),
                        pltpu.VMEM((D_XATT, tn), BF16)],
        compiler_params=_params(("arbitrary", "arbitrary"), 52),
        name="merge",
    )(u_a, nb, ox, qg, qg, qg, w_a, w_b, w_x)


def _resid_norm_kernel(xp_ref, xs_ref, y_ref, g_post_ref, g_pre_ref, x1_ref, hn_ref, *, n_prompt_blocks):
    i = pl.program_id(0)

    def body(x):
        x1 = x + _rms_scale(y_ref[...], g_post_ref[...])
        x1_ref[...] = x1
        hn_ref[...] = _rms_scale(x1, g_pre_ref[...]).astype(hn_ref.dtype)

    @pl.when(i < n_prompt_blocks)
    def _():
        body(xp_ref[...])

    @pl.when(i >= n_prompt_blocks)
    def _():
        body(xs_ref[...])


def _resid_norm(xp, xs, y, g_post, g_pre, *, tm=512):
    n_p, n_s = xp.shape[0] // tm, xs.shape[0] // tm
    d = xp.shape[1]
    return pl.pallas_call(
        functools.partial(_resid_norm_kernel, n_prompt_blocks=n_p),
        grid=(n_p + n_s,),
        in_specs=[
            pl.BlockSpec((tm, d), lambda i: (jnp.minimum(i, n_p - 1), 0)),
            pl.BlockSpec((tm, d), lambda i: (jnp.maximum(i - n_p, 0), 0)),
            pl.BlockSpec((tm, d), lambda i: (i, 0)),
            pl.BlockSpec((1, d), lambda i: (0, 0)),
            pl.BlockSpec((1, d), lambda i: (0, 0)),
        ],
        out_specs=[pl.BlockSpec((tm, d), lambda i: (i, 0)), pl.BlockSpec((tm, d), lambda i: (i, 0))],
        out_shape=[jax.ShapeDtypeStruct((T_ALL, d), F32), jax.ShapeDtypeStruct((T_ALL, d), BF16)],
        compiler_params=_params(("arbitrary",), 48),
        name="resid_norm",
    )(xp, xs, y, g_post, g_pre)


def _resid_out_kernel(x_ref, y_ref, g_ref, o_ref):
    o_ref[...] = x_ref[...] + _rms_scale(y_ref[...], g_ref[...])


def _resid_out(x1, y, g, *, row0, rows, tm=512):
    d = x1.shape[1]
    b0 = row0 // tm
    return pl.pallas_call(
        _resid_out_kernel,
        grid=(rows // tm,),
        in_specs=[
            pl.BlockSpec((tm, d), lambda i: (b0 + i, 0)),
            pl.BlockSpec((tm, d), lambda i: (b0 + i, 0)),
            pl.BlockSpec((1, d), lambda i: (0, 0)),
        ],
        out_specs=pl.BlockSpec((tm, d), lambda i: (i, 0)),
        out_shape=jax.ShapeDtypeStruct((rows, d), F32),
        compiler_params=_params(("arbitrary",), 40),
        name="resid_out",
    )(x1, y, g)


def _head_expand_matrix():
    h = lax.broadcasted_iota(jnp.int32, (3 * HEAD_LANES, D_INNER), 0) % HEAD_LANES
    ch = lax.broadcasted_iota(jnp.int32, (3 * HEAD_LANES, D_INNER), 1) // SSM_HEAD_DIM
    return (h == ch).astype(BF16)


def _group_sum_matrix():
    rows = 2 * SSM_GROUPS * D_STATE
    g = (lax.broadcasted_iota(jnp.int32, (rows, HEAD_LANES), 0) % (SSM_GROUPS * D_STATE)) // D_STATE
    h = lax.broadcasted_iota(jnp.int32, (rows, HEAD_LANES), 1)
    return ((h // HEADS_PER_GROUP == g) & (h < SSM_HEADS)).astype(BF16)


def _pad_lanes(v, width):
    return jnp.pad(v, ((0, 0), (0, width - v.shape[1])))


def kernel(x_prompt, x_sample, mem_prompt, cache_mem_k, cache_mem_v, state_conv_a, state_conv_b, state_ssm, norm_mix_pre, norm_mix_post, norm_mlp_pre, norm_mlp_post, norm_mem, w_in, conv_a_w, w_out_a, conv_b_w, conv_b_bias, dt_bias, a_log, d_skip, ssm_norm_w, w_out_b, w_mem_kv, w_out_x, w_o, w_ff1, w_ff2):
    assert w_in.shape == (1, D_MODEL, OFF_GATES + 3 * D_MODEL)
    assert x_prompt.shape == (BATCH, SEQ, D_MODEL) and x_sample.shape == (DEC_BATCH, DEC_SEQ, D_MODEL)

    xp = x_prompt.reshape(T_PROMPT, D_MODEL)
    xs = x_sample.reshape(T_SAMPLE, D_MODEL)
    w_in2 = w_in[0]

    mem_n = _norm_cast(mem_prompt.reshape(BATCH * N_MEM, D_MODEL), norm_mem)
    kv = _matmul(mem_n, w_mem_kv[0], n_out=2 * D_XATT, tm=BATCH * N_MEM, tn=512, vmem_mb=40, name="mem_kv")
    k_p = kv[:, :D_XATT].reshape(BATCH, N_MEM, D_XATT)
    v_p = kv[:, D_XATT:].reshape(BATCH, N_MEM, D_XATT)

    xn = _norm_cast2(xp, xs, norm_mix_pre)
    proj = _matmul(xn, w_in2, n_out=N_MAIN, tm=1024, tn=1024, vmem_mb=48, name="in_proj_main")
    qg = _matmul(xn, w_in2[:, OFF_Q:], n_out=N_QG, tm=1024, tn=512, vmem_mb=40, name="in_proj_qg")
    dt = _dt_proj(xn, _pad_lanes(w_in2[:, OFF_DT:OFF_Q], HEAD_LANES), _pad_lanes(dt_bias, HEAD_LANES))

    hist_a = jnp.pad(state_conv_a[0], ((0, 0), (DEC_SEQ - (CONV_A_W - 1), 0), (0, 0)))
    u_a, conv_a_p, conv_a_s = _conv_a(proj, conv_a_w[0], hist_a)

    hist_b = jnp.pad(state_conv_b[0], ((0, 0), (DEC_SEQ - (CONV_B_W - 1), 0), (0, 0)))
    xbc, conv_b_p, conv_b_s = _conv_b(proj, conv_b_w[0], conv_b_bias, hist_b)
    e3 = _head_expand_matrix()
    gsum = _group_sum_matrix()
    a_log_w = _pad_lanes(a_log, HEAD_LANES)
    d_skip_e = jnp.repeat(d_skip, SSM_HEAD_DIM, axis=1)
    nb, ssm_p = _ssd_prompt(xbc, dt, proj, a_log_w, d_skip_e, ssm_norm_w, e3)
    nb, ssm_s = _ssd_sample(xbc, dt, proj, state_ssm[0], a_log_w, d_skip_e, ssm_norm_w, e3, gsum, nb)

    k_s = cache_mem_k[0].reshape(DEC_BATCH, N_MEM, D_XATT)
    v_s = cache_mem_v[0].reshape(DEC_BATCH, N_MEM, D_XATT)
    ox = _attention(qg, k_p, v_p, k_s, v_s)

    merged = _merge(u_a, nb, ox, qg, w_out_a[0], w_out_b[0], w_out_x[0])
    mixed = _matmul(merged, w_o[0], n_out=D_MODEL, tm=1024, tn=1024, vmem_mb=48, name="w_o")
    x1, hn = _resid_norm(xp, xs, mixed, norm_mix_post, norm_mlp_pre)

    hff = _matmul(hn, w_ff1[0], n_out=D_FF, tm=1024, tn=1024, vmem_mb=48, out_dtype=BF16, epilogue="relu2",
                  name="ff1")
    ff = _matmul(hff, w_ff2[0], n_out=D_MODEL, tm=512, tn=512, vmem_mb=52, single_buffer_w=True, name="ff2")
    y_p = _resid_out(x1, ff, norm_mlp_post, row0=0, rows=T_PROMPT)
    y_s = _resid_out(x1, ff, norm_mlp_post, row0=T_PROMPT, rows=T_SAMPLE)

    return (
        y_p.reshape(BATCH, SEQ, D_MODEL),
        y_s.reshape(DEC_BATCH, DEC_SEQ, D_MODEL),
        k_p.reshape(1, BATCH, N_MEM, X_HEADS, X_HEAD_DIM),
        v_p.reshape(1, BATCH, N_MEM, X_HEADS, X_HEAD_DIM),
        conv_a_p[None],
        conv_b_p[None],
        ssm_p[None],
        conv_a_s[None],
        conv_b_s[None],
        ssm_s[None],
    )
```

```python
import functools

import jax
import jax.numpy as jnp
from jax import lax
from jax.experimental import pallas as pl
from jax.experimental.pallas import tpu as pltpu

F32 = jnp.float32
BF16 = jnp.bfloat16

D_MODEL = 2048
BATCH = 4
SEQ = 2048
DEC_BATCH = 128
DEC_SEQ = 8
N_MEM = 256
X_HEADS = 4
X_HEAD_DIM = 128
D_XATT = X_HEADS * X_HEAD_DIM
D_CONV_A = D_MODEL
CONV_A_W = 3
D_INNER = 2 * D_MODEL
SSM_HEAD_DIM = 64
SSM_HEADS = D_INNER // SSM_HEAD_DIM
SSM_GROUPS = 8
HEADS_PER_GROUP = SSM_HEADS // SSM_GROUPS
GROUP_WIDTH = D_INNER // SSM_GROUPS
D_STATE = 128
CONV_B_W = 4
SSD_CHUNK = 128
D_XBC = D_INNER + 2 * SSM_GROUPS * D_STATE
D_FF = 4 * D_MODEL
EPS = 1e-6

T_PROMPT = BATCH * SEQ
T_SAMPLE = DEC_BATCH * DEC_SEQ
T_ALL = T_PROMPT + T_SAMPLE

OFF_BA = 0
OFF_CA = OFF_BA + D_CONV_A
OFF_HA = OFF_CA + D_CONV_A
OFF_Z = OFF_HA + D_CONV_A
OFF_XBC = OFF_Z + D_INNER
OFF_DT = OFF_XBC + D_XBC
OFF_Q = OFF_DT + SSM_HEADS
OFF_GATES = OFF_Q + D_XATT
N_MAIN = OFF_DT
N_QG = D_XATT + 3 * D_MODEL

SUBLANES = 8
LANES = 128
HEAD_LANES = LANES


def _params(semantics, vmem_mb):
    return pltpu.CompilerParams(dimension_semantics=semantics, vmem_limit_bytes=vmem_mb << 20)


def _rms_scale(x, g):
    return x * lax.rsqrt(jnp.mean(x * x, axis=-1, keepdims=True) + EPS) * g


def _softmax_rows(s):
    e = jnp.exp(s - jnp.max(s, axis=-1, keepdims=True))
    return e / jnp.sum(e, axis=-1, keepdims=True)


def _split3(v):
    hi = v.astype(BF16)
    r1 = v - hi.astype(F32)
    mid = r1.astype(BF16)
    lo = (r1 - mid.astype(F32)).astype(BF16)
    return jnp.concatenate([hi, mid, lo], axis=1)


def _norm_cast2_kernel(xp_ref, xs_ref, g_ref, o_ref, *, n_prompt_blocks):
    i = pl.program_id(0)

    @pl.when(i < n_prompt_blocks)
    def _():
        o_ref[...] = _rms_scale(xp_ref[...], g_ref[...]).astype(o_ref.dtype)

    @pl.when(i >= n_prompt_blocks)
    def _():
        o_ref[...] = _rms_scale(xs_ref[...], g_ref[...]).astype(o_ref.dtype)


def _norm_cast2(xp, xs, g, *, tm=512):
    n_p, n_s = xp.shape[0] // tm, xs.shape[0] // tm
    d = xp.shape[1]
    return pl.pallas_call(
        functools.partial(_norm_cast2_kernel, n_prompt_blocks=n_p),
        grid=(n_p + n_s,),
        in_specs=[
            pl.BlockSpec((tm, d), lambda i: (jnp.minimum(i, n_p - 1), 0)),
            pl.BlockSpec((tm, d), lambda i: (jnp.maximum(i - n_p, 0), 0)),
            pl.BlockSpec((1, d), lambda i: (0, 0)),
        ],
        out_specs=pl.BlockSpec((tm, d), lambda i: (i, 0)),
        out_shape=jax.ShapeDtypeStruct((xp.shape[0] + xs.shape[0], d), BF16),
        compiler_params=_params(("arbitrary",), 32),
        name="norm_cast2",
    )(xp, xs, g)


def _norm_cast_kernel(x_ref, g_ref, o_ref):
    o_ref[...] = _rms_scale(x_ref[...], g_ref[...]).astype(o_ref.dtype)


def _norm_cast(x, g, *, tm=512):
    m, d = x.shape
    return pl.pallas_call(
        _norm_cast_kernel,
        grid=(m // tm,),
        in_specs=[pl.BlockSpec((tm, d), lambda i: (i, 0)), pl.BlockSpec((1, d), lambda i: (0, 0))],
        out_specs=pl.BlockSpec((tm, d), lambda i: (i, 0)),
        out_shape=jax.ShapeDtypeStruct((m, d), BF16),
        compiler_params=_params(("arbitrary",), 32),
        name="norm_cast",
    )(x, g)


def _mm_kernel(a_ref, w_ref, o_ref, wb_ref, *, epilogue):
    @pl.when(pl.program_id(1) == 0)
    def _():
        wb_ref[...] = w_ref[...].astype(BF16)

    acc = jnp.dot(a_ref[...], wb_ref[...], preferred_element_type=F32)
    if epilogue == "relu2":
        acc = jnp.square(jnp.maximum(acc, 0.0))
    o_ref[...] = acc.astype(o_ref.dtype)


def _matmul(a, w, *, n_out, tm, tn, vmem_mb, col0=0, out_dtype=F32, epilogue=None,
            single_buffer_w=False, name="matmul"):
    m, k = a.shape
    assert col0 % tn == 0
    w_kwargs = {"pipeline_mode": pl.Buffered(1)} if single_buffer_w else {}
    return pl.pallas_call(
        functools.partial(_mm_kernel, epilogue=epilogue),
        grid=(n_out // tn, m // tm),
        in_specs=[
            pl.BlockSpec((tm, k), lambda n, i: (i, 0)),
            pl.BlockSpec((k, tn), lambda n, i: (0, col0 // tn + n), **w_kwargs),
        ],
        out_specs=pl.BlockSpec((tm, tn), lambda n, i: (i, n)),
        out_shape=jax.ShapeDtypeStruct((m, n_out), out_dtype),
        scratch_shapes=[pltpu.VMEM((k, tn), BF16)],
        compiler_params=_params(("arbitrary", "arbitrary"), vmem_mb),
        name=name,
    )(a, w)


def _mm_shift_kernel(a_ref, w_ref, wn_ref, o_ref, wb_ref, *, shift):
    @pl.when(pl.program_id(1) == 0)
    def _():
        tn = w_ref.shape[1]
        w = jnp.concatenate([w_ref[...], wn_ref[...]], axis=1)
        wb_ref[...] = w[:, shift:shift + tn].astype(BF16)

    o_ref[...] = jnp.dot(a_ref[...], wb_ref[...], preferred_element_type=F32).astype(o_ref.dtype)


def _matmul_cols(a, w, *, col0, n_out, tm, tn, vmem_mb, name):
    m, k = a.shape
    shift = col0 % LANES
    base = col0 - shift
    assert base % tn == 0 and n_out % tn == 0 and 0 < shift < LANES
    return pl.pallas_call(
        functools.partial(_mm_shift_kernel, shift=shift),
        grid=(n_out // tn, m // tm),
        in_specs=[
            pl.BlockSpec((tm, k), lambda n, i: (i, 0)),
            pl.BlockSpec((k, tn), lambda n, i: (0, base // tn + n)),
            pl.BlockSpec((k, LANES), lambda n, i: (0, (base + tn * (n + 1)) // LANES)),
        ],
        out_specs=pl.BlockSpec((tm, tn), lambda n, i: (i, n)),
        out_shape=jax.ShapeDtypeStruct((m, n_out), F32),
        scratch_shapes=[pltpu.VMEM((k, tn), BF16)],
        compiler_params=_params(("arbitrary", "arbitrary"), vmem_mb),
        name=name,
    )(a, w, w)


def _dt_kernel(a_ref, w_ref, b_ref, o_ref):
    raw = jnp.dot(a_ref[...], w_ref[...].astype(BF16), preferred_element_type=F32) + b_ref[...]
    dt = jnp.maximum(raw, 0.0) + jnp.log1p(jnp.exp(-jnp.abs(raw)))
    lane = lax.broadcasted_iota(jnp.int32, dt.shape, 1)
    o_ref[...] = jnp.where(lane < SSM_HEADS, dt, 0.0)


def _dt_proj(xn, w_in2, dt_bias, *, tm=1024):
    m, k = xn.shape
    return pl.pallas_call(
        _dt_kernel,
        grid=(m // tm,),
        in_specs=[
            pl.BlockSpec((tm, k), lambda i: (i, 0)),
            pl.BlockSpec((k, HEAD_LANES), lambda i: (0, OFF_DT // HEAD_LANES)),
            pl.BlockSpec((1, HEAD_LANES), lambda i: (0, 0)),
        ],
        out_specs=pl.BlockSpec((tm, HEAD_LANES), lambda i: (i, 0)),
        out_shape=jax.ShapeDtypeStruct((m, HEAD_LANES), F32),
        compiler_params=_params(("arbitrary",), 32),
        name="dt_proj",
    )(xn, w_in2, dt_bias)


TM_CONV = 1024
TILES_PER_SEQ = SEQ // TM_CONV
SAMPLE_TILE = T_PROMPT // TM_CONV


def _causal_taps_rows(u, ext_ref, w, taps, first_of_seq):
    n = u.shape[0]

    @pl.when(first_of_seq)
    def _():
        ext_ref[0:SUBLANES, :] = jnp.zeros((SUBLANES, u.shape[1]), F32)

    ext_ref[SUBLANES:SUBLANES + n, :] = u
    acc = None
    for j in range(taps):
        d = taps - 1 - j
        term = ext_ref[SUBLANES - d:SUBLANES - d + n, :] * w[j:j + 1, :]
        acc = term if acc is None else acc + term
    return acc


def _keep_tail(ext_ref, n):
    ext_ref[0:SUBLANES, :] = ext_ref[n:n + SUBLANES, :]


def _causal_taps_3d(u, hist, w, taps):
    c = u.shape[-1]
    t = lax.broadcasted_iota(jnp.int32, u.shape, 1)
    acc = None
    for j in range(taps):
        d = taps - 1 - j
        if d == 0:
            term = u
        else:
            term = jnp.where(t >= d, pltpu.roll(u, d, axis=1), pltpu.roll(hist, d, axis=1))
        term = term * w[j:j + 1, :].reshape(1, 1, c)
        acc = term if acc is None else acc + term
    return acc


def _mixer_a_kernel(a_ref, wb_ref, wc_ref, wh_ref, cw_ref, hist_ref, u_ref, stp_ref, sts_ref,
                    w16_ref, ext_ref):
    i = pl.program_id(1)
    tm, tn = u_ref.shape
    k = CONV_A_W - 1

    @pl.when(i == 0)
    def _():
        w16_ref[0] = wb_ref[...].astype(BF16)
        w16_ref[1] = wc_ref[...].astype(BF16)
        w16_ref[2] = wh_ref[...].astype(BF16)

    a = a_ref[...]
    b = jnp.dot(a, w16_ref[0], preferred_element_type=F32)
    ch = (jnp.dot(a, w16_ref[1], preferred_element_type=F32)
          * jnp.dot(a, w16_ref[2], preferred_element_type=F32))

    @pl.when(i < SAMPLE_TILE)
    def _():
        conv = _causal_taps_rows(ch, ext_ref, cw_ref[...], CONV_A_W, i % TILES_PER_SEQ == 0)
        u_ref[...] = (b * conv).astype(u_ref.dtype)
        _keep_tail(ext_ref, tm)

        @pl.when(i % TILES_PER_SEQ == TILES_PER_SEQ - 1)
        def _():
            stp_ref[0] = ext_ref[SUBLANES + tm - k:SUBLANES + tm, :]

    @pl.when(i == SAMPLE_TILE)
    def _():
        ch3 = ch.reshape(DEC_BATCH, DEC_SEQ, tn)
        conv = _causal_taps_3d(ch3, hist_ref[...], cw_ref[...], CONV_A_W)
        u = b.reshape(DEC_BATCH, DEC_SEQ, tn) * conv
        u_ref[...] = u.reshape(tm, tn).astype(u_ref.dtype)
        sts_ref[...] = ch3[:, DEC_SEQ - k:, :]


def _in_proj_mixer_a(xn, w_in2, conv_w, hist_s, *, tn=512):
    tm = TM_CONV
    kdim = xn.shape[1]
    k = CONV_A_W - 1
    last_seq = BATCH - 1
    return pl.pallas_call(
        _mixer_a_kernel,
        grid=(D_CONV_A // tn, T_ALL // tm),
        in_specs=[
            pl.BlockSpec((tm, kdim), lambda n, i: (i, 0)),
            pl.BlockSpec((kdim, tn), lambda n, i: (0, OFF_BA // tn + n), pipeline_mode=pl.Buffered(1)),
            pl.BlockSpec((kdim, tn), lambda n, i: (0, OFF_CA // tn + n), pipeline_mode=pl.Buffered(1)),
            pl.BlockSpec((kdim, tn), lambda n, i: (0, OFF_HA // tn + n), pipeline_mode=pl.Buffered(1)),
            pl.BlockSpec((CONV_A_W, tn), lambda n, i: (0, n)),
            pl.BlockSpec((DEC_BATCH, DEC_SEQ, tn), lambda n, i: (0, 0, n)),
        ],
        out_specs=[
            pl.BlockSpec((tm, tn), lambda n, i: (i, n)),
            pl.BlockSpec((1, k, tn), lambda n, i: (jnp.minimum(i // TILES_PER_SEQ, last_seq), 0, n)),
            pl.BlockSpec((DEC_BATCH, k, tn), lambda n, i: (0, 0, n)),
        ],
        out_shape=[
            jax.ShapeDtypeStruct((T_ALL, D_CONV_A), BF16),
            jax.ShapeDtypeStruct((BATCH, k, D_CONV_A), F32),
            jax.ShapeDtypeStruct((DEC_BATCH, k, D_CONV_A), F32),
        ],
        scratch_shapes=[pltpu.VMEM((3, kdim, tn), BF16), pltpu.VMEM((SUBLANES + tm, tn), F32)],
        compiler_params=_params(("arbitrary", "arbitrary"), 56),
        name="in_proj_mixer_a",
    )(xn, w_in2, w_in2, w_in2, conv_w, hist_s)


def _xbc_kernel(a_ref, w_ref, cw_ref, cbias_ref, hist_ref, o_ref, stp_ref, sts_ref, w16_ref, ext_ref):
    i = pl.program_id(1)
    tm, tn = o_ref.shape
    k = CONV_B_W - 1

    @pl.when(i == 0)
    def _():
        w16_ref[...] = w_ref[...].astype(BF16)

    raw = jnp.dot(a_ref[...], w16_ref[...], preferred_element_type=F32)

    @pl.when(i < SAMPLE_TILE)
    def _():
        conv = _causal_taps_rows(raw, ext_ref, cw_ref[...], CONV_B_W, i % TILES_PER_SEQ == 0) + cbias_ref[...]
        o_ref[...] = conv * jax.nn.sigmoid(conv)
        _keep_tail(ext_ref, tm)

        @pl.when(i % TILES_PER_SEQ == TILES_PER_SEQ - 1)
        def _():
            stp_ref[0] = ext_ref[SUBLANES + tm - k:SUBLANES + tm, :]

    @pl.when(i == SAMPLE_TILE)
    def _():
        raw3 = raw.reshape(DEC_BATCH, DEC_SEQ, tn)
        conv = _causal_taps_3d(raw3, hist_ref[...], cw_ref[...], CONV_B_W) + cbias_ref[...].reshape(1, 1, tn)
        o_ref[...] = (conv * jax.nn.sigmoid(conv)).reshape(tm, tn)
        sts_ref[...] = raw3[:, DEC_SEQ - k:, :]


def _in_proj_xbc(xn, w_in2, conv_w, conv_bias, hist_s, *, tn=512):
    tm = TM_CONV
    kdim = xn.shape[1]
    k = CONV_B_W - 1
    last_seq = BATCH - 1
    return pl.pallas_call(
        _xbc_kernel,
        grid=(D_XBC // tn, T_ALL // tm),
        in_specs=[
            pl.BlockSpec((tm, kdim), lambda n, i: (i, 0)),
            pl.BlockSpec((kdim, tn), lambda n, i: (0, OFF_XBC // tn + n)),
            pl.BlockSpec((CONV_B_W, tn), lambda n, i: (0, n)),
            pl.BlockSpec((1, tn), lambda n, i: (0, n)),
            pl.BlockSpec((DEC_BATCH, DEC_SEQ, tn), lambda n, i: (0, 0, n)),
        ],
        out_specs=[
            pl.BlockSpec((tm, tn), lambda n, i: (i, n)),
            pl.BlockSpec((1, k, tn), lambda n, i: (jnp.minimum(i // TILES_PER_SEQ, last_seq), 0, n)),
            pl.BlockSpec((DEC_BATCH, k, tn), lambda n, i: (0, 0, n)),
        ],
        out_shape=[
            jax.ShapeDtypeStruct((T_ALL, D_XBC), F32),
            jax.ShapeDtypeStruct((BATCH, k, D_XBC), F32),
            jax.ShapeDtypeStruct((DEC_BATCH, k, D_XBC), F32),
        ],
        scratch_shapes=[pltpu.VMEM((kdim, tn), BF16), pltpu.VMEM((SUBLANES + tm, tn), F32)],
        compiler_params=_params(("arbitrary", "arbitrary"), 48),
        name="in_proj_xbc",
    )(xn, w_in2, conv_w, conv_bias, hist_s)


def _gated_group_norm(y, z, w):
    u = y * (z * jax.nn.sigmoid(z))
    return u * lax.rsqrt(jnp.mean(u * u, axis=-1, keepdims=True) + EPS) * w


def _ssd_prompt_kernel(xs_ref, b_ref, c_ref, dt_ref, z_ref, alog_ref, dskip_ref,
                       nw_ref, e3_ref, nb_ref, st_ref, state_ref, *, n_chunks):
    q = SSD_CHUNK
    ci = pl.program_id(1)

    @pl.when(ci == 0)
    def _():
        state_ref[...] = jnp.zeros_like(state_ref)

    row = lax.broadcasted_iota(jnp.int32, (q, q), 0)
    col = lax.broadcasted_iota(jnp.int32, (q, q), 1)
    causal = row >= col
    dt = dt_ref[...]
    da = dt * (-jnp.exp(alog_ref[...]))
    acum = jnp.dot(causal.astype(F32), da, preferred_element_type=F32,
                   precision=lax.Precision.HIGHEST)
    acum_t = acum.T
    a_last = acum[q - 1:q, :]
    per_head = jnp.concatenate(
        [dt, jnp.exp(a_last - acum), jnp.exp(acum), jnp.broadcast_to(jnp.exp(a_last), (SUBLANES, HEAD_LANES))],
        axis=0)
    wide = jnp.dot(_split3(per_head), e3_ref[...], preferred_element_type=F32)
    dt_e, dend_e, eac_e, cd_e = wide[0:q], wide[q:2 * q], wide[2 * q:3 * q], wide[3 * q:3 * q + 1]

    lane = lax.broadcasted_iota(jnp.int32, (q, LANES), 1)
    first_head = lane < SSM_HEAD_DIM
    for g in range(SSM_GROUPS):
        gs = slice(g * GROUP_WIDTH, (g + 1) * GROUP_WIDTH)
        ns = slice(g * D_STATE, (g + 1) * D_STATE)
        xg = xs_ref[:, gs]
        xdt = xg * dt_e[:, gs]
        bg = b_ref[:, ns].astype(BF16)
        cg = c_ref[:, ns].astype(BF16)
        cb = lax.dot_general(cg, bg, (((1,), (1,)), ((), ())), preferred_element_type=F32)
        s_prev = state_ref[g]
        y_off = jnp.dot(cg, s_prev.astype(BF16), preferred_element_type=F32)
        xw = (xdt * dend_e[:, gs]).astype(BF16)
        s_add = lax.dot_general(bg, xw, (((0,), (0,)), ((), ())), preferred_element_type=F32)
        state_ref[g] = s_prev * cd_e[:, gs] + s_add
        pairs = []
        for k in range(HEADS_PER_GROUP // 2):
            h0 = g * HEADS_PER_GROUP + 2 * k
            ms = []
            for h in (h0, h0 + 1):
                seg = acum[:, h:h + 1] - acum_t[h:h + 1, :]
                decay = jnp.exp(jnp.where(causal, seg, -jnp.inf))
                ms.append((cb * decay).astype(BF16))
            lhs = jnp.concatenate(ms, axis=1)
            xp = xdt[:, k * LANES:(k + 1) * LANES]
            rhs = jnp.concatenate([jnp.where(first_head, xp, 0.0).astype(BF16),
                                   jnp.where(first_head, 0.0, xp).astype(BF16)], axis=0)
            pairs.append(jnp.dot(lhs, rhs, preferred_element_type=F32))
        y_diag = jnp.concatenate(pairs, axis=1)
        y = (y_diag + y_off * eac_e[:, gs]) + dskip_ref[:, gs] * xg
        nb_ref[:, gs] = _gated_group_norm(y, z_ref[:, gs], nw_ref[:, gs]).astype(nb_ref.dtype)

    @pl.when(ci == n_chunks - 1)
    def _():
        for g in range(SSM_GROUPS):
            st_ref[0, g * HEADS_PER_GROUP:(g + 1) * HEADS_PER_GROUP] = (
                state_ref[g].T.reshape(HEADS_PER_GROUP, SSM_HEAD_DIM, D_STATE))


def _ssd_prompt(xbc, dt, z, a_log, d_skip_e, norm_w, e3):
    q = SSD_CHUNK
    n_chunks = SEQ // q
    bc = SSM_GROUPS * D_STATE
    row = lambda b, c: b * n_chunks + c
    const = lambda b, c: (0, 0)
    return pl.pallas_call(
        functools.partial(_ssd_prompt_kernel, n_chunks=n_chunks),
        grid=(BATCH, n_chunks),
        in_specs=[
            pl.BlockSpec((q, D_INNER), lambda b, c: (row(b, c), 0)),
            pl.BlockSpec((q, bc), lambda b, c: (row(b, c), D_INNER // bc)),
            pl.BlockSpec((q, bc), lambda b, c: (row(b, c), D_INNER // bc + 1)),
            pl.BlockSpec((q, HEAD_LANES), lambda b, c: (row(b, c), 0)),
            pl.BlockSpec((q, D_INNER), lambda b, c: (row(b, c), 0)),
            pl.BlockSpec((1, HEAD_LANES), const),
            pl.BlockSpec((1, D_INNER), const),
            pl.BlockSpec((1, D_INNER), const),
            pl.BlockSpec((3 * HEAD_LANES, D_INNER), const),
        ],
        out_specs=[
            pl.BlockSpec((q, D_INNER), lambda b, c: (row(b, c), 0)),
            pl.BlockSpec((1, SSM_HEADS, SSM_HEAD_DIM, D_STATE), lambda b, c: (b, 0, 0, 0)),
        ],
        out_shape=[
            jax.ShapeDtypeStruct((T_ALL, D_INNER), BF16),
            jax.ShapeDtypeStruct((BATCH, SSM_HEADS, SSM_HEAD_DIM, D_STATE), F32),
        ],
        scratch_shapes=[pltpu.VMEM((SSM_GROUPS, D_STATE, GROUP_WIDTH), F32)],
        compiler_params=_params(("arbitrary", "arbitrary"), 48),
        name="ssd_prompt",
    )(xbc, xbc, xbc, dt, z, a_log, d_skip_e, norm_w, e3)


def _ssd_sample_kernel(xs_ref, b_ref, c_ref, dt_ref, z_ref, h0_ref, alog_ref, dskip_ref,
                       nw_ref, e3_ref, gsum_ref, alias_ref, nb_ref, hout_ref, y_ref, *, seqs):
    del alias_ref
    q = DEC_SEQ
    tri = (lax.broadcasted_iota(jnp.int32, (q, q), 0) >= lax.broadcasted_iota(jnp.int32, (q, q), 1)).astype(F32)
    qi = lax.broadcasted_iota(jnp.int32, (q, HEAD_LANES), 0)
    neg_a = -jnp.exp(alog_ref[...])
    for j in range(seqs):
        rows = slice(j * q, (j + 1) * q)
        dt = dt_ref[rows, :]
        acum = jnp.dot(tri, dt * neg_a, preferred_element_type=F32, precision=lax.Precision.HIGHEST)
        a_last = acum[q - 1:q, :]
        cd = jnp.exp(a_last)
        cb16 = c_ref[rows, :].astype(BF16)
        bb16 = b_ref[rows, :].astype(BF16)
        cf = cb16.astype(F32)
        bf = bb16.astype(F32)
        decays, prods = [], []
        for s in range(q):
            seg = acum - acum[s:s + 1, :]
            decays.append(jnp.exp(jnp.where(qi >= s, seg, -jnp.inf)))
            prods.append(cf * bf[s:s + 1, :])
        decay = jnp.concatenate(decays, axis=0)
        prod = jnp.concatenate(prods, axis=0)
        p_hi = prod.astype(BF16)
        p_lo = (prod - p_hi.astype(F32)).astype(BF16)
        cbx = jnp.dot(jnp.concatenate([p_hi, p_lo], axis=1), gsum_ref[...], preferred_element_type=F32)
        per_head = jnp.concatenate(
            [cbx * decay, dt, jnp.exp(a_last - acum), jnp.exp(acum), jnp.broadcast_to(cd, (SUBLANES, HEAD_LANES))],
            axis=0)
        wide = jnp.dot(_split3(per_head), e3_ref[...], preferred_element_type=F32)
        n0 = q * q
        w_e = wide[0:n0]
        dt_e, dend_e, eac_e = wide[n0:n0 + q], wide[n0 + q:n0 + 2 * q], wide[n0 + 2 * q:n0 + 3 * q]
        x = xs_ref[rows, :]
        xdt = x * dt_e
        y_diag = None
        for s in range(q):
            term = w_e[s * q:(s + 1) * q] * xdt[s:s + 1, :]
            y_diag = term if y_diag is None else y_diag + term
        xw = (xdt * dend_e).astype(BF16)
        for g in range(SSM_GROUPS):
            gs = slice(g * GROUP_WIDTH, (g + 1) * GROUP_WIDTH)
            ns = slice(g * D_STATE, (g + 1) * D_STATE)
            hs = slice(g * HEADS_PER_GROUP, (g + 1) * HEADS_PER_GROUP)
            h_prev = h0_ref[j, hs].reshape(GROUP_WIDTH, D_STATE)
            y_off = lax.dot_general(cb16[:, ns], h_prev.astype(BF16), (((1,), (1,)), ((), ())),
                                    preferred_element_type=F32)
            s_add = lax.dot_general(xw[:, gs], bb16[:, ns], (((0,), (0,)), ((), ())),
                                    preferred_element_type=F32)
            for r in range(HEADS_PER_GROUP):
                h = g * HEADS_PER_GROUP + r
                hout_ref[j, h] = (h0_ref[j, h] * cd[:, h:h + 1]
                                  + s_add[r * SSM_HEAD_DIM:(r + 1) * SSM_HEAD_DIM, :])
            y = (y_diag[:, gs] + y_off * eac_e[:, gs]) + dskip_ref[:, gs] * x[:, gs]
            y_ref[rows, gs] = _gated_group_norm(y, z_ref[rows, gs], nw_ref[:, gs])
    nb_ref[...] = y_ref[...].astype(nb_ref.dtype)


def _ssd_sample(xbc, dt, z, h0, a_log, d_skip_e, norm_w, e3, gsum, nb, *, seqs=2):
    rows = seqs * DEC_SEQ
    r0 = T_PROMPT // rows
    bc = SSM_GROUPS * D_STATE
    return pl.pallas_call(
        functools.partial(_ssd_sample_kernel, seqs=seqs),
        grid=(DEC_BATCH // seqs,),
        in_specs=[
            pl.BlockSpec((rows, D_INNER), lambda i: (r0 + i, 0)),
            pl.BlockSpec((rows, bc), lambda i: (r0 + i, D_INNER // bc)),
            pl.BlockSpec((rows, bc), lambda i: (r0 + i, D_INNER // bc + 1)),
            pl.BlockSpec((rows, HEAD_LANES), lambda i: (r0 + i, 0)),
            pl.BlockSpec((rows, D_INNER), lambda i: (r0 + i, 0)),
            pl.BlockSpec((seqs, SSM_HEADS, SSM_HEAD_DIM, D_STATE), lambda i: (i, 0, 0, 0)),
            pl.BlockSpec((1, HEAD_LANES), lambda i: (0, 0)),
            pl.BlockSpec((1, D_INNER), lambda i: (0, 0)),
            pl.BlockSpec((1, D_INNER), lambda i: (0, 0)),
            pl.BlockSpec((3 * HEAD_LANES, D_INNER), lambda i: (0, 0)),
            pl.BlockSpec((2 * bc, HEAD_LANES), lambda i: (0, 0)),
            pl.BlockSpec(memory_space=pl.ANY),
        ],
        out_specs=[
            pl.BlockSpec((rows, D_INNER), lambda i: (r0 + i, 0)),
            pl.BlockSpec((seqs, SSM_HEADS, SSM_HEAD_DIM, D_STATE), lambda i: (i, 0, 0, 0)),
        ],
        out_shape=[
            jax.ShapeDtypeStruct((T_ALL, D_INNER), BF16),
            jax.ShapeDtypeStruct((DEC_BATCH, SSM_HEADS, SSM_HEAD_DIM, D_STATE), F32),
        ],
        scratch_shapes=[pltpu.VMEM((rows, D_INNER), F32)],
        input_output_aliases={11: 0},
        compiler_params=_params(("arbitrary",), 48),
        name="ssd_sample",
    )(xbc, xbc, xbc, dt, z, h0, a_log, d_skip_e, norm_w, e3, gsum, nb)


def _attn_body(q_ref, k_ref, v_ref, o_ref, *, seqs, tq):
    scale = X_HEAD_DIM ** -0.5
    outs = []
    for j in range(seqs):
        q = q_ref[j * tq:(j + 1) * tq, :]
        k = k_ref[j]
        v = v_ref[j]
        heads = []
        for h in range(X_HEADS):
            hs = slice(h * X_HEAD_DIM, (h + 1) * X_HEAD_DIM)
            s = lax.dot_general(q[:, hs].astype(BF16), k[:, hs].astype(BF16), (((1,), (1,)), ((), ())),
                                preferred_element_type=F32) * scale
            p = _softmax_rows(s)
            heads.append(jnp.dot(p.astype(BF16), v[:, hs].astype(BF16), preferred_element_type=F32))
        outs.append(jnp.concatenate(heads, axis=1))
    o = outs[0] if seqs == 1 else jnp.concatenate(outs, axis=0)
    o_ref[...] = o.astype(o_ref.dtype)


def _attn_prompt_kernel(q_ref, k_ref, v_ref, o_ref, *, seqs, tq):
    _attn_body(q_ref, k_ref, v_ref, o_ref, seqs=seqs, tq=tq)


def _attn_sample_kernel(q_ref, k_ref, v_ref, alias_ref, o_ref, *, seqs):
    del alias_ref
    scale = X_HEAD_DIM ** -0.5
    shape = (X_HEADS * DEC_SEQ, N_MEM * X_HEADS)
    own_head = (lax.broadcasted_iota(jnp.int32, shape, 1) % X_HEADS
                == lax.broadcasted_iota(jnp.int32, shape, 0) // DEC_SEQ)
    outs = []
    for j in range(seqs):
        q = q_ref[j * DEC_SEQ:(j + 1) * DEC_SEQ, :]
        q_rows = jnp.concatenate([q[:, h * X_HEAD_DIM:(h + 1) * X_HEAD_DIM] for h in range(X_HEADS)], axis=0)
        s = lax.dot_general(q_rows.astype(BF16), k_ref[j].astype(BF16), (((1,), (1,)), ((), ())),
                            preferred_element_type=F32) * scale
        p = _softmax_rows(jnp.where(own_head, s, -jnp.inf))
        o_rows = jnp.dot(p.astype(BF16), v_ref[j].astype(BF16), preferred_element_type=F32)
        outs.append(jnp.concatenate([o_rows[h * DEC_SEQ:(h + 1) * DEC_SEQ] for h in range(X_HEADS)], axis=1))
    o_ref[...] = jnp.concatenate(outs, axis=0).astype(o_ref.dtype)


def _attention(qg, k_p, v_p, k_s, v_s, *, tq=512, seqs=8):
    nq = SEQ // tq
    o = pl.pallas_call(
        functools.partial(_attn_prompt_kernel, seqs=1, tq=tq),
        grid=(BATCH, nq),
        in_specs=[
            pl.BlockSpec((tq, D_XATT), lambda b, i: (b * nq + i, 0)),
            pl.BlockSpec((1, N_MEM, D_XATT), lambda b, i: (b, 0, 0)),
            pl.BlockSpec((1, N_MEM, D_XATT), lambda b, i: (b, 0, 0)),
        ],
        out_specs=pl.BlockSpec((tq, D_XATT), lambda b, i: (b * nq + i, 0)),
        out_shape=jax.ShapeDtypeStruct((T_ALL, D_XATT), BF16),
        compiler_params=_params(("arbitrary", "arbitrary"), 32),
        name="attn_prompt",
    )(qg, k_p, v_p)
    rows = seqs * DEC_SEQ
    r0 = T_PROMPT // rows
    return pl.pallas_call(
        functools.partial(_attn_sample_kernel, seqs=seqs),
        grid=(DEC_BATCH // seqs,),
        in_specs=[
            pl.BlockSpec((rows, D_XATT), lambda i: (r0 + i, 0)),
            pl.BlockSpec((seqs, N_MEM * X_HEADS, X_HEAD_DIM), lambda i: (i, 0, 0)),
            pl.BlockSpec((seqs, N_MEM * X_HEADS, X_HEAD_DIM), lambda i: (i, 0, 0)),
            pl.BlockSpec(memory_space=pl.ANY),
        ],
        out_specs=pl.BlockSpec((rows, D_XATT), lambda i: (r0 + i, 0)),
        out_shape=jax.ShapeDtypeStruct((T_ALL, D_XATT), BF16),
        input_output_aliases={3: 0},
        compiler_params=_params(("arbitrary",), 40),
        name="attn_sample",
    )(qg, k_s, v_s, o)


def _merge_kernel(ua_ref, nb_ref, ox_ref, ga_ref, gb_ref, gx_ref, wa_ref, wb_ref, wx_ref, o_ref,
                  wa16_ref, wb16_ref, wx16_ref):
    @pl.when(pl.program_id(1) == 0)
    def _():
        wa16_ref[...] = wa_ref[...].astype(BF16)
        wb16_ref[...] = wb_ref[...].astype(BF16)
        wx16_ref[...] = wx_ref[...].astype(BF16)

    ya = jnp.dot(ua_ref[...], wa16_ref[...], preferred_element_type=F32)
    yb = jnp.dot(nb_ref[...], wb16_ref[...], preferred_element_type=F32)
    yx = jnp.dot(ox_ref[...], wx16_ref[...], preferred_element_type=F32)
    merged = (jax.nn.sigmoid(ga_ref[...]) * ya + jax.nn.sigmoid(gb_ref[...]) * yb) + jax.nn.sigmoid(gx_ref[...]) * yx
    o_ref[...] = merged.astype(o_ref.dtype)


def _merge(u_a, nb, ox, qg, w_a, w_b, w_x, *, tm=512, tn=512):
    g0 = D_XATT // tn
    gstep = D_MODEL // tn
    once = {"pipeline_mode": pl.Buffered(1)}
    return pl.pallas_call(
        _merge_kernel,
        grid=(D_MODEL // tn, T_ALL // tm),
        in_specs=[
            pl.BlockSpec((tm, D_CONV_A), lambda n, i: (i, 0)),
            pl.BlockSpec((tm, D_INNER), lambda n, i: (i, 0)),
            pl.BlockSpec((tm, D_XATT), lambda n, i: (i, 0)),
            pl.BlockSpec((tm, tn), lambda n, i: (i, g0 + n)),
            pl.BlockSpec((tm, tn), lambda n, i: (i, g0 + gstep + n)),
            pl.BlockSpec((tm, tn), lambda n, i: (i, g0 + 2 * gstep + n)),
            pl.BlockSpec((D_CONV_A, tn), lambda n, i: (0, n), **once),
            pl.BlockSpec((D_INNER, tn), lambda n, i: (0, n), **once),
            pl.BlockSpec((D_XATT, tn), lambda n, i: (0, n), **once),
        ],
        out_specs=pl.BlockSpec((tm, tn), lambda n, i: (i, n)),
        out_shape=jax.ShapeDtypeStruct((T_ALL, D_MODEL), BF16),
        scratch_shapes=[pltpu.VMEM((D_CONV_A, tn), BF16), pltpu.VMEM((D_INNER, tn), BF16),
                        pltpu.VMEM((D_XATT, tn), BF16)],
        compiler_params=_params(("arbitrary", "arbitrary"), 52),
        name="merge",
    )(u_a, nb, ox, qg, qg, qg, w_a, w_b, w_x)


def _out_proj_resid_kernel(a_ref, w_ref, xp_ref, xs_ref, g_post_ref, g_pre_ref, x1_ref, hn_ref, wb_ref,
                           *, n_prompt_blocks):
    i = pl.program_id(0)

    @pl.when(i == 0)
    def _():
        wb_ref[...] = w_ref[...].astype(BF16)

    def body(x):
        y = jnp.dot(a_ref[...], wb_ref[...], preferred_element_type=F32)
        x1 = x + _rms_scale(y, g_post_ref[...])
        x1_ref[...] = x1
        hn_ref[...] = _rms_scale(x1, g_pre_ref[...]).astype(hn_ref.dtype)

    @pl.when(i < n_prompt_blocks)
    def _():
        body(xp_ref[...])

    @pl.when(i >= n_prompt_blocks)
    def _():
        body(xs_ref[...])


def _out_proj_resid(a, w, xp, xs, g_post, g_pre, *, tm=256):
    n_p, n_s = xp.shape[0] // tm, xs.shape[0] // tm
    d = xp.shape[1]
    k = a.shape[1]
    return pl.pallas_call(
        functools.partial(_out_proj_resid_kernel, n_prompt_blocks=n_p),
        grid=(n_p + n_s,),
        in_specs=[
            pl.BlockSpec((tm, k), lambda i: (i, 0)),
            pl.BlockSpec((k, d), lambda i: (0, 0), pipeline_mode=pl.Buffered(1)),
            pl.BlockSpec((tm, d), lambda i: (jnp.minimum(i, n_p - 1), 0)),
            pl.BlockSpec((tm, d), lambda i: (jnp.maximum(i - n_p, 0), 0)),
            pl.BlockSpec((1, d), lambda i: (0, 0)),
            pl.BlockSpec((1, d), lambda i: (0, 0)),
        ],
        out_specs=[pl.BlockSpec((tm, d), lambda i: (i, 0)), pl.BlockSpec((tm, d), lambda i: (i, 0))],
        out_shape=[jax.ShapeDtypeStruct((T_ALL, d), F32), jax.ShapeDtypeStruct((T_ALL, d), BF16)],
        scratch_shapes=[pltpu.VMEM((k, d), BF16)],
        compiler_params=_params(("arbitrary",), 52),
        name="out_proj_resid",
    )(a, w, xp, xs, g_post, g_pre)


def _resid_out_kernel(x_ref, y_ref, g_ref, o_ref):
    o_ref[...] = x_ref[...] + _rms_scale(y_ref[...], g_ref[...])


def _resid_out(x1, y, g, *, row0, rows, tm=512):
    d = x1.shape[1]
    b0 = row0 // tm
    return pl.pallas_call(
        _resid_out_kernel,
        grid=(rows // tm,),
        in_specs=[
            pl.BlockSpec((tm, d), lambda i: (b0 + i, 0)),
            pl.BlockSpec((tm, d), lambda i: (b0 + i, 0)),
            pl.BlockSpec((1, d), lambda i: (0, 0)),
        ],
        out_specs=pl.BlockSpec((tm, d), lambda i: (i, 0)),
        out_shape=jax.ShapeDtypeStruct((rows, d), F32),
        compiler_params=_params(("arbitrary",), 40),
        name="resid_out",
    )(x1, y, g)


def _head_expand_matrix():
    h = lax.broadcasted_iota(jnp.int32, (3 * HEAD_LANES, D_INNER), 0) % HEAD_LANES
    ch = lax.broadcasted_iota(jnp.int32, (3 * HEAD_LANES, D_INNER), 1) // SSM_HEAD_DIM
    return (h == ch).astype(BF16)


def _group_sum_matrix():
    rows = 2 * SSM_GROUPS * D_STATE
    g = (lax.broadcasted_iota(jnp.int32, (rows, HEAD_LANES), 0) % (SSM_GROUPS * D_STATE)) // D_STATE
    h = lax.broadcasted_iota(jnp.int32, (rows, HEAD_LANES), 1)
    return ((h // HEADS_PER_GROUP == g) & (h < SSM_HEADS)).astype(BF16)


def _pad_lanes(v, width):
    return jnp.pad(v, ((0, 0), (0, width - v.shape[1])))


def kernel(x_prompt, x_sample, mem_prompt, cache_mem_k, cache_mem_v, state_conv_a, state_conv_b, state_ssm, norm_mix_pre, norm_mix_post, norm_mlp_pre, norm_mlp_post, norm_mem, w_in, conv_a_w, w_out_a, conv_b_w, conv_b_bias, dt_bias, a_log, d_skip, ssm_norm_w, w_out_b, w_mem_kv, w_out_x, w_o, w_ff1, w_ff2):
    assert w_in.shape == (1, D_MODEL, OFF_GATES + 3 * D_MODEL)
    assert x_prompt.shape == (BATCH, SEQ, D_MODEL) and x_sample.shape == (DEC_BATCH, DEC_SEQ, D_MODEL)

    xp = x_prompt.reshape(T_PROMPT, D_MODEL)
    xs = x_sample.reshape(T_SAMPLE, D_MODEL)
    w_in2 = w_in[0]

    mem_n = _norm_cast(mem_prompt.reshape(BATCH * N_MEM, D_MODEL), norm_mem)
    kv = _matmul(mem_n, w_mem_kv[0], n_out=2 * D_XATT, tm=BATCH * N_MEM, tn=512, vmem_mb=40, name="mem_kv")
    k_p = kv[:, :D_XATT].reshape(BATCH, N_MEM, D_XATT)
    v_p = kv[:, D_XATT:].reshape(BATCH, N_MEM, D_XATT)

    xn = _norm_cast2(xp, xs, norm_mix_pre)
    z = _matmul(xn, w_in2, col0=OFF_Z, n_out=D_INNER, tm=1024, tn=1024, vmem_mb=48, name="in_proj_z")
    qg = _matmul_cols(xn, w_in2, col0=OFF_Q, n_out=N_QG, tm=1024, tn=512, vmem_mb=40, name="in_proj_qg")
    dt = _dt_proj(xn, w_in2, _pad_lanes(dt_bias, HEAD_LANES))

    hist_a = jnp.pad(state_conv_a[0], ((0, 0), (DEC_SEQ - (CONV_A_W - 1), 0), (0, 0)))
    u_a, conv_a_p, conv_a_s = _in_proj_mixer_a(xn, w_in2, conv_a_w[0], hist_a)

    hist_b = jnp.pad(state_conv_b[0], ((0, 0), (DEC_SEQ - (CONV_B_W - 1), 0), (0, 0)))
    xbc, conv_b_p, conv_b_s = _in_proj_xbc(xn, w_in2, conv_b_w[0], conv_b_bias, hist_b)
    e3 = _head_expand_matrix()
    gsum = _group_sum_matrix()
    a_log_w = _pad_lanes(a_log, HEAD_LANES)
    d_skip_e = jnp.repeat(d_skip, SSM_HEAD_DIM, axis=1)
    nb, ssm_p = _ssd_prompt(xbc, dt, z, a_log_w, d_skip_e, ssm_norm_w, e3)
    nb, ssm_s = _ssd_sample(xbc, dt, z, state_ssm[0], a_log_w, d_skip_e, ssm_norm_w, e3, gsum, nb)

    k_s = cache_mem_k[0].reshape(DEC_BATCH, N_MEM * X_HEADS, X_HEAD_DIM)
    v_s = cache_mem_v[0].reshape(DEC_BATCH, N_MEM * X_HEADS, X_HEAD_DIM)
    ox = _attention(qg, k_p, v_p, k_s, v_s)

    merged = _merge(u_a, nb, ox, qg, w_out_a[0], w_out_b[0], w_out_x[0])
    x1, hn = _out_proj_resid(merged, w_o[0], xp, xs, norm_mix_post, norm_mlp_pre)

    hff = _matmul(hn, w_ff1[0], n_out=D_FF, tm=1024, tn=1024, vmem_mb=48, out_dtype=BF16, epilogue="relu2",
                  name="ff1")
    ff = _matmul(hff, w_ff2[0], n_out=D_MODEL, tm=512, tn=512, vmem_mb=52, single_buffer_w=True, name="ff2")
    y_p = _resid_out(x1, ff, norm_mlp_post, row0=0, rows=T_PROMPT)
    y_s = _resid_out(x1, ff, norm_mlp_post, row0=T_PROMPT, rows=T_SAMPLE)

    return (
        y_p.reshape(BATCH, SEQ, D_MODEL),
        y_s.reshape(DEC_BATCH, DEC_SEQ, D_MODEL),
        k_p.reshape(1, BATCH, N_MEM, X_HEADS, X_HEAD_DIM),
        v_p.reshape(1, BATCH, N_MEM, X_HEADS, X_HEAD_DIM),
        conv_a_p[None],
        conv_b_p[None],
        ssm_p[None],
        conv_a_s[None],
        conv_b_s[None],
        ssm_s[None],
    )
```

```python
import functools

import jax
import jax.numpy as jnp
from jax import lax
from jax.experimental import pallas as pl
from jax.experimental.pallas import tpu as pltpu

F32 = jnp.float32
BF16 = jnp.bfloat16

D_MODEL = 2048
BATCH = 4
SEQ = 2048
DEC_BATCH = 128
DEC_SEQ = 8
N_MEM = 256
X_HEADS = 4
X_HEAD_DIM = 128
D_XATT = X_HEADS * X_HEAD_DIM
D_CONV_A = D_MODEL
CONV_A_W = 3
D_INNER = 2 * D_MODEL
SSM_HEAD_DIM = 64
SSM_HEADS = D_INNER // SSM_HEAD_DIM
SSM_GROUPS = 8
HEADS_PER_GROUP = SSM_HEADS // SSM_GROUPS
GROUP_WIDTH = D_INNER // SSM_GROUPS
D_STATE = 128
CONV_B_W = 4
SSD_CHUNK = 128
D_XBC = D_INNER + 2 * SSM_GROUPS * D_STATE
D_FF = 4 * D_MODEL
EPS = 1e-6

T_PROMPT = BATCH * SEQ
T_SAMPLE = DEC_BATCH * DEC_SEQ
T_ALL = T_PROMPT + T_SAMPLE

OFF_BA = 0
OFF_CA = OFF_BA + D_CONV_A
OFF_HA = OFF_CA + D_CONV_A
OFF_Z = OFF_HA + D_CONV_A
OFF_XBC = OFF_Z + D_INNER
OFF_DT = OFF_XBC + D_XBC
OFF_Q = OFF_DT + SSM_HEADS
OFF_GATES = OFF_Q + D_XATT
N_MAIN = OFF_DT
N_QG = D_XATT + 3 * D_MODEL

SUBLANES = 8
LANES = 128
HEAD_LANES = LANES


def _params(semantics, vmem_mb):
    return pltpu.CompilerParams(dimension_semantics=semantics, vmem_limit_bytes=vmem_mb << 20)


def _rms_scale(x, g):
    return x * lax.rsqrt(jnp.mean(x * x, axis=-1, keepdims=True) + EPS) * g


def _softmax_rows(s):
    e = jnp.exp(s - jnp.max(s, axis=-1, keepdims=True))
    return e / jnp.sum(e, axis=-1, keepdims=True)


def _split3(v):
    hi = v.astype(BF16)
    r1 = v - hi.astype(F32)
    mid = r1.astype(BF16)
    lo = (r1 - mid.astype(F32)).astype(BF16)
    return jnp.concatenate([hi, mid, lo], axis=1)


def _norm_cast2_kernel(xp_ref, xs_ref, g_ref, o_ref, *, n_prompt_blocks):
    i = pl.program_id(0)

    @pl.when(i < n_prompt_blocks)
    def _():
        o_ref[...] = _rms_scale(xp_ref[...], g_ref[...]).astype(o_ref.dtype)

    @pl.when(i >= n_prompt_blocks)
    def _():
        o_ref[...] = _rms_scale(xs_ref[...], g_ref[...]).astype(o_ref.dtype)


def _norm_cast2(xp, xs, g, *, tm=512):
    n_p, n_s = xp.shape[0] // tm, xs.shape[0] // tm
    d = xp.shape[1]
    return pl.pallas_call(
        functools.partial(_norm_cast2_kernel, n_prompt_blocks=n_p),
        grid=(n_p + n_s,),
        in_specs=[
            pl.BlockSpec((tm, d), lambda i: (jnp.minimum(i, n_p - 1), 0)),
            pl.BlockSpec((tm, d), lambda i: (jnp.maximum(i - n_p, 0), 0)),
            pl.BlockSpec((1, d), lambda i: (0, 0)),
        ],
        out_specs=pl.BlockSpec((tm, d), lambda i: (i, 0)),
        out_shape=jax.ShapeDtypeStruct((xp.shape[0] + xs.shape[0], d), BF16),
        compiler_params=_params(("arbitrary",), 32),
        name="norm_cast2",
    )(xp, xs, g)


def _norm_cast_kernel(x_ref, g_ref, o_ref):
    o_ref[...] = _rms_scale(x_ref[...], g_ref[...]).astype(o_ref.dtype)


def _norm_cast(x, g, *, tm=512):
    m, d = x.shape
    return pl.pallas_call(
        _norm_cast_kernel,
        grid=(m // tm,),
        in_specs=[pl.BlockSpec((tm, d), lambda i: (i, 0)), pl.BlockSpec((1, d), lambda i: (0, 0))],
        out_specs=pl.BlockSpec((tm, d), lambda i: (i, 0)),
        out_shape=jax.ShapeDtypeStruct((m, d), BF16),
        compiler_params=_params(("arbitrary",), 32),
        name="norm_cast",
    )(x, g)


def _mm_kernel(a_ref, w_ref, o_ref, wb_ref, *, epilogue):
    @pl.when(pl.program_id(1) == 0)
    def _():
        wb_ref[...] = w_ref[...].astype(BF16)

    acc = jnp.dot(a_ref[...], wb_ref[...], preferred_element_type=F32)
    if epilogue == "relu2":
        acc = jnp.square(jnp.maximum(acc, 0.0))
    o_ref[...] = acc.astype(o_ref.dtype)


def _matmul(a, w, *, n_out, tm, tn, vmem_mb, col0=0, out_dtype=F32, epilogue=None,
            single_buffer_w=False, name="matmul"):
    m, k = a.shape
    assert col0 % tn == 0
    w_kwargs = {"pipeline_mode": pl.Buffered(1)} if single_buffer_w else {}
    return pl.pallas_call(
        functools.partial(_mm_kernel, epilogue=epilogue),
        grid=(n_out // tn, m // tm),
        in_specs=[
            pl.BlockSpec((tm, k), lambda n, i: (i, 0)),
            pl.BlockSpec((k, tn), lambda n, i: (0, col0 // tn + n), **w_kwargs),
        ],
        out_specs=pl.BlockSpec((tm, tn), lambda n, i: (i, n)),
        out_shape=jax.ShapeDtypeStruct((m, n_out), out_dtype),
        scratch_shapes=[pltpu.VMEM((k, tn), BF16)],
        compiler_params=_params(("arbitrary", "arbitrary"), vmem_mb),
        name=name,
    )(a, w)


NT_DIMS = (((1,), (1,)), ((), ()))
ROW_UNIT = 64


def _mm_nt_kernel(a_ref, w_ref, o_ref, wb_ref):
    @pl.when(pl.program_id(1) == 0)
    def _():
        wb_ref[...] = w_ref[...].astype(BF16)

    o_ref[...] = lax.dot_general(a_ref[...], wb_ref[...], NT_DIMS, preferred_element_type=F32)


def _matmul_nt(a, wt, *, row0, n_out, tm, tn, vmem_mb, name):
    m, k = a.shape
    assert row0 % tn == 0 and n_out % tn == 0
    return pl.pallas_call(
        _mm_nt_kernel,
        grid=(n_out // tn, m // tm),
        in_specs=[
            pl.BlockSpec((tm, k), lambda n, i: (i, 0)),
            pl.BlockSpec((tn, k), lambda n, i: (row0 // tn + n, 0)),
        ],
        out_specs=pl.BlockSpec((tm, tn), lambda n, i: (i, n)),
        out_shape=jax.ShapeDtypeStruct((m, n_out), F32),
        scratch_shapes=[pltpu.VMEM((tn, k), BF16)],
        compiler_params=_params(("arbitrary", "arbitrary"), vmem_mb),
        name=name,
    )(a, wt)


def _gates_kernel(a_ref, wa_ref, wb_ref, wc_ref, o_ref, w16_ref, *, skip):
    @pl.when(pl.program_id(1) == 0)
    def _():
        w = jnp.concatenate([wa_ref[skip:, :], wb_ref[...], wc_ref[...]], axis=0)
        w16_ref[...] = w.astype(BF16)

    o_ref[...] = lax.dot_general(a_ref[...], w16_ref[...], NT_DIMS, preferred_element_type=F32)


def _in_proj_gates(xn, wt, *, tm=1024, tn=1024):
    m, k = xn.shape
    half = tn // 2
    skip = OFF_GATES % half
    rest = skip
    assert skip % ROW_UNIT == 0 and rest == ROW_UNIT
    a0 = OFF_GATES // half
    return pl.pallas_call(
        functools.partial(_gates_kernel, skip=skip),
        grid=(3 * D_MODEL // tn, m // tm),
        in_specs=[
            pl.BlockSpec((tm, k), lambda n, i: (i, 0)),
            pl.BlockSpec((half, k), lambda n, i: (a0 + 2 * n, 0)),
            pl.BlockSpec((half, k), lambda n, i: (a0 + 2 * n + 1, 0)),
            pl.BlockSpec((ROW_UNIT, k), lambda n, i: ((a0 + 2 * n + 2) * (half // ROW_UNIT), 0)),
        ],
        out_specs=pl.BlockSpec((tm, tn), lambda n, i: (i, n)),
        out_shape=jax.ShapeDtypeStruct((m, 3 * D_MODEL), F32),
        scratch_shapes=[pltpu.VMEM((tn, k), BF16)],
        compiler_params=_params(("arbitrary", "arbitrary"), 48),
        name="in_proj_gates",
    )(xn, wt, wt, wt)


def _q_dt_kernel(a_ref, wa_ref, wb_ref, bias_ref, q_ref, dt_ref, w16_ref):
    @pl.when(pl.program_id(0) == 0)
    def _():
        w = jnp.concatenate([wa_ref[SSM_HEADS:, :], wb_ref[...], wa_ref[0:SSM_HEADS, :],
                             jnp.zeros((HEAD_LANES - SSM_HEADS, wa_ref.shape[1]), F32)], axis=0)
        w16_ref[...] = w.astype(BF16)

    out = lax.dot_general(a_ref[...], w16_ref[...], NT_DIMS, preferred_element_type=F32)
    q_ref[...] = out[:, :D_XATT]
    raw = out[:, D_XATT:] + bias_ref[...]
    dt = jnp.maximum(raw, 0.0) + jnp.log1p(jnp.exp(-jnp.abs(raw)))
    lane = lax.broadcasted_iota(jnp.int32, dt.shape, 1)
    dt_ref[...] = jnp.where(lane < SSM_HEADS, dt, 0.0)


def _in_proj_q_dt(xn, wt, dt_bias, *, tm=1024):
    m, k = xn.shape
    assert OFF_DT % D_XATT == 0 and OFF_Q - OFF_DT == SSM_HEADS == ROW_UNIT
    return pl.pallas_call(
        _q_dt_kernel,
        grid=(m // tm,),
        in_specs=[
            pl.BlockSpec((tm, k), lambda i: (i, 0)),
            pl.BlockSpec((D_XATT, k), lambda i: (OFF_DT // D_XATT, 0)),
            pl.BlockSpec((ROW_UNIT, k), lambda i: ((OFF_DT + D_XATT) // ROW_UNIT, 0)),
            pl.BlockSpec((1, HEAD_LANES), lambda i: (0, 0)),
        ],
        out_specs=[
            pl.BlockSpec((tm, D_XATT), lambda i: (i, 0)),
            pl.BlockSpec((tm, HEAD_LANES), lambda i: (i, 0)),
        ],
        out_shape=[
            jax.ShapeDtypeStruct((m, D_XATT), F32),
            jax.ShapeDtypeStruct((m, HEAD_LANES), F32),
        ],
        scratch_shapes=[pltpu.VMEM((D_XATT + HEAD_LANES, k), BF16)],
        compiler_params=_params(("arbitrary",), 40),
        name="in_proj_q_dt",
    )(xn, wt, wt, dt_bias)


TM_CONV = 1024
TILES_PER_SEQ = SEQ // TM_CONV
SAMPLE_TILE = T_PROMPT // TM_CONV
XBC_SUB_BLOCKS = 4


def _causal_taps_rows(u, ext_ref, w, taps, first_of_seq):
    n = u.shape[0]

    @pl.when(first_of_seq)
    def _():
        ext_ref[0:SUBLANES, :] = jnp.zeros((SUBLANES, u.shape[1]), F32)

    ext_ref[SUBLANES:SUBLANES + n, :] = u
    acc = None
    for j in range(taps):
        d = taps - 1 - j
        term = ext_ref[SUBLANES - d:SUBLANES - d + n, :] * w[j:j + 1, :]
        acc = term if acc is None else acc + term
    return acc


def _keep_tail(ext_ref, n):
    ext_ref[0:SUBLANES, :] = ext_ref[n:n + SUBLANES, :]


def _causal_taps_3d(u, hist, w, taps):
    c = u.shape[-1]
    t = lax.broadcasted_iota(jnp.int32, u.shape, 1)
    acc = None
    for j in range(taps):
        d = taps - 1 - j
        if d == 0:
            term = u
        else:
            term = jnp.where(t >= d, pltpu.roll(u, d, axis=1), pltpu.roll(hist, d, axis=1))
        term = term * w[j:j + 1, :].reshape(1, 1, c)
        acc = term if acc is None else acc + term
    return acc


def _mixer_a_kernel(a_ref, wb_ref, wc_ref, wh_ref, cw_ref, hist_ref, u_ref, stp_ref, sts_ref,
                    w16_ref, ext_ref):
    i = pl.program_id(1)
    tm, tn = u_ref.shape
    k = CONV_A_W - 1

    @pl.when(i == 0)
    def _():
        w16_ref[0] = wb_ref[...].astype(BF16)
        w16_ref[1] = wc_ref[...].astype(BF16)
        w16_ref[2] = wh_ref[...].astype(BF16)

    a = a_ref[...]
    b = lax.dot_general(a, w16_ref[0], NT_DIMS, preferred_element_type=F32)
    ch = (lax.dot_general(a, w16_ref[1], NT_DIMS, preferred_element_type=F32)
          * lax.dot_general(a, w16_ref[2], NT_DIMS, preferred_element_type=F32))

    @pl.when(i < SAMPLE_TILE)
    def _():
        conv = _causal_taps_rows(ch, ext_ref, cw_ref[...], CONV_A_W, i % TILES_PER_SEQ == 0)
        u_ref[...] = (b * conv).astype(u_ref.dtype)
        _keep_tail(ext_ref, tm)

        @pl.when(i % TILES_PER_SEQ == TILES_PER_SEQ - 1)
        def _():
            stp_ref[0] = ext_ref[SUBLANES + tm - k:SUBLANES + tm, :]

    @pl.when(i == SAMPLE_TILE)
    def _():
        ch3 = ch.reshape(DEC_BATCH, DEC_SEQ, tn)
        conv = _causal_taps_3d(ch3, hist_ref[...], cw_ref[...], CONV_A_W)
        u = b.reshape(DEC_BATCH, DEC_SEQ, tn) * conv
        u_ref[...] = u.reshape(tm, tn).astype(u_ref.dtype)
        sts_ref[...] = ch3[:, DEC_SEQ - k:, :]


def _in_proj_mixer_a(xn, wt, conv_w, hist_s, *, tn=512):
    tm = TM_CONV
    kdim = xn.shape[1]
    k = CONV_A_W - 1
    last_seq = BATCH - 1
    return pl.pallas_call(
        _mixer_a_kernel,
        grid=(D_CONV_A // tn, T_ALL // tm),
        in_specs=[
            pl.BlockSpec((tm, kdim), lambda n, i: (i, 0)),
            pl.BlockSpec((tn, kdim), lambda n, i: (OFF_BA // tn + n, 0), pipeline_mode=pl.Buffered(1)),
            pl.BlockSpec((tn, kdim), lambda n, i: (OFF_CA // tn + n, 0), pipeline_mode=pl.Buffered(1)),
            pl.BlockSpec((tn, kdim), lambda n, i: (OFF_HA // tn + n, 0), pipeline_mode=pl.Buffered(1)),
            pl.BlockSpec((CONV_A_W, tn), lambda n, i: (0, n)),
            pl.BlockSpec((DEC_BATCH, DEC_SEQ, tn), lambda n, i: (0, 0, n)),
        ],
        out_specs=[
            pl.BlockSpec((tm, tn), lambda n, i: (i, n)),
            pl.BlockSpec((1, k, tn), lambda n, i: (jnp.minimum(i // TILES_PER_SEQ, last_seq), 0, n)),
            pl.BlockSpec((DEC_BATCH, k, tn), lambda n, i: (0, 0, n)),
        ],
        out_shape=[
            jax.ShapeDtypeStruct((T_ALL, D_CONV_A), BF16),
            jax.ShapeDtypeStruct((BATCH, k, D_CONV_A), F32),
            jax.ShapeDtypeStruct((DEC_BATCH, k, D_CONV_A), F32),
        ],
        scratch_shapes=[pltpu.VMEM((3, tn, kdim), BF16), pltpu.VMEM((SUBLANES + tm, tn), F32)],
        compiler_params=_params(("arbitrary", "arbitrary"), 56),
        name="in_proj_mixer_a",
    )(xn, wt, wt, wt, conv_w, hist_s)


def _xbc_kernel(a_ref, w_ref, cw_ref, cbias_ref, hist_ref, o_ref, stp_ref, sts_ref, w16_ref, ext_ref):
    i = pl.program_id(1)
    tm, tn = o_ref.shape
    k = CONV_B_W - 1

    @pl.when(i == 0)
    def _():
        w16_ref[...] = w_ref[...].astype(BF16)

    @pl.when(i < SAMPLE_TILE)
    def _():
        @pl.when(i % TILES_PER_SEQ == 0)
        def _():
            ext_ref[0, 0:SUBLANES, :] = jnp.zeros((SUBLANES, tn), F32)

        sub = tm // XBC_SUB_BLOCKS
        cw = cw_ref[...]
        bias = cbias_ref[...]

        def project(s):
            ext_ref[s % 2, SUBLANES:SUBLANES + sub, :] = lax.dot_general(
                a_ref[s * sub:(s + 1) * sub, :], w16_ref[...], NT_DIMS, preferred_element_type=F32)

        project(0)
        for s in range(XBC_SUB_BLOCKS):
            if s + 1 < XBC_SUB_BLOCKS:
                project(s + 1)
            acc = None
            for j in range(CONV_B_W):
                d = CONV_B_W - 1 - j
                term = ext_ref[s % 2, SUBLANES - d:SUBLANES - d + sub, :] * cw[j:j + 1, :]
                acc = term if acc is None else acc + term
            conv = acc + bias
            o_ref[s * sub:(s + 1) * sub, :] = conv * jax.nn.sigmoid(conv)
            ext_ref[(s + 1) % 2, 0:SUBLANES, :] = ext_ref[s % 2, sub:sub + SUBLANES, :]

        @pl.when(i % TILES_PER_SEQ == TILES_PER_SEQ - 1)
        def _():
            last = (XBC_SUB_BLOCKS - 1) % 2
            stp_ref[0] = ext_ref[last, SUBLANES + sub - k:SUBLANES + sub, :]

    @pl.when(i == SAMPLE_TILE)
    def _():
        raw = lax.dot_general(a_ref[...], w16_ref[...], NT_DIMS, preferred_element_type=F32)
        raw3 = raw.reshape(DEC_BATCH, DEC_SEQ, tn)
        conv = _causal_taps_3d(raw3, hist_ref[...], cw_ref[...], CONV_B_W) + cbias_ref[...].reshape(1, 1, tn)
        o_ref[...] = (conv * jax.nn.sigmoid(conv)).reshape(tm, tn)
        sts_ref[...] = raw3[:, DEC_SEQ - k:, :]


def _in_proj_xbc(xn, wt, conv_w, conv_bias, hist_s, *, tn=512):
    tm = TM_CONV
    kdim = xn.shape[1]
    k = CONV_B_W - 1
    last_seq = BATCH - 1
    return pl.pallas_call(
        _xbc_kernel,
        grid=(D_XBC // tn, T_ALL // tm),
        in_specs=[
            pl.BlockSpec((tm, kdim), lambda n, i: (i, 0)),
            pl.BlockSpec((tn, kdim), lambda n, i: (OFF_XBC // tn + n, 0)),
            pl.BlockSpec((CONV_B_W, tn), lambda n, i: (0, n)),
            pl.BlockSpec((1, tn), lambda n, i: (0, n)),
            pl.BlockSpec((DEC_BATCH, DEC_SEQ, tn), lambda n, i: (0, 0, n)),
        ],
        out_specs=[
            pl.BlockSpec((tm, tn), lambda n, i: (i, n)),
            pl.BlockSpec((1, k, tn), lambda n, i: (jnp.minimum(i // TILES_PER_SEQ, last_seq), 0, n)),
            pl.BlockSpec((DEC_BATCH, k, tn), lambda n, i: (0, 0, n)),
        ],
        out_shape=[
            jax.ShapeDtypeStruct((T_ALL, D_XBC), F32),
            jax.ShapeDtypeStruct((BATCH, k, D_XBC), F32),
            jax.ShapeDtypeStruct((DEC_BATCH, k, D_XBC), F32),
        ],
        scratch_shapes=[pltpu.VMEM((tn, kdim), BF16),
                        pltpu.VMEM((2, SUBLANES + tm // XBC_SUB_BLOCKS, tn), F32)],
        compiler_params=_params(("arbitrary", "arbitrary"), 48),
        name="in_proj_xbc",
    )(xn, wt, conv_w, conv_bias, hist_s)


def _gated_group_norm(y, z, w):
    u = y * (z * jax.nn.sigmoid(z))
    return u * lax.rsqrt(jnp.mean(u * u, axis=-1, keepdims=True) + EPS) * w


def _ssd_prompt_kernel(xs_ref, b_ref, c_ref, dt_ref, z_ref, alog_ref, dskip_ref,
                       nw_ref, e3_ref, nb_ref, st_ref, state_ref, *, n_chunks):
    q = SSD_CHUNK
    ci = pl.program_id(1)

    @pl.when(ci == 0)
    def _():
        state_ref[...] = jnp.zeros_like(state_ref)

    row = lax.broadcasted_iota(jnp.int32, (q, q), 0)
    col = lax.broadcasted_iota(jnp.int32, (q, q), 1)
    causal = row >= col
    dt = dt_ref[...]
    da = dt * (-jnp.exp(alog_ref[...]))
    acum = jnp.dot(causal.astype(F32), da, preferred_element_type=F32,
                   precision=lax.Precision.HIGHEST)
    acum_t = acum.T
    a_last = acum[q - 1:q, :]
    per_head = jnp.concatenate(
        [dt, jnp.exp(a_last - acum), jnp.exp(acum), jnp.broadcast_to(jnp.exp(a_last), (SUBLANES, HEAD_LANES))],
        axis=0)
    wide = jnp.dot(_split3(per_head), e3_ref[...], preferred_element_type=F32)
    dt_e, dend_e, eac_e, cd_e = wide[0:q], wide[q:2 * q], wide[2 * q:3 * q], wide[3 * q:3 * q + 1]

    lane = lax.broadcasted_iota(jnp.int32, (q, LANES), 1)
    first_head = lane < SSM_HEAD_DIM
    for g in range(SSM_GROUPS):
        gs = slice(g * GROUP_WIDTH, (g + 1) * GROUP_WIDTH)
        ns = slice(g * D_STATE, (g + 1) * D_STATE)
        xg = xs_ref[:, gs]
        xdt = xg * dt_e[:, gs]
        bg = b_ref[:, ns].astype(BF16)
        cg = c_ref[:, ns].astype(BF16)
        cb = lax.dot_general(cg, bg, (((1,), (1,)), ((), ())), preferred_element_type=F32)
        s_prev = state_ref[g]
        y_off = jnp.dot(cg, s_prev.astype(BF16), preferred_element_type=F32)
        xw = (xdt * dend_e[:, gs]).astype(BF16)
        s_add = lax.dot_general(bg, xw, (((0,), (0,)), ((), ())), preferred_element_type=F32)
        state_ref[g] = s_prev * cd_e[:, gs] + s_add
        pairs = []
        for k in range(HEADS_PER_GROUP // 2):
            h0 = g * HEADS_PER_GROUP + 2 * k
            ms = []
            for h in (h0, h0 + 1):
                seg = acum[:, h:h + 1] - acum_t[h:h + 1, :]
                decay = jnp.exp(jnp.where(causal, seg, -jnp.inf))
                ms.append((cb * decay).astype(BF16))
            lhs = jnp.concatenate(ms, axis=1)
            xp = xdt[:, k * LANES:(k + 1) * LANES]
            rhs = jnp.concatenate([jnp.where(first_head, xp, 0.0).astype(BF16),
                                   jnp.where(first_head, 0.0, xp).astype(BF16)], axis=0)
            pairs.append(jnp.dot(lhs, rhs, preferred_element_type=F32))
        y_diag = jnp.concatenate(pairs, axis=1)
        y = (y_diag + y_off * eac_e[:, gs]) + dskip_ref[:, gs] * xg
        nb_ref[:, gs] = _gated_group_norm(y, z_ref[:, gs], nw_ref[:, gs]).astype(nb_ref.dtype)

    @pl.when(ci == n_chunks - 1)
    def _():
        for g in range(SSM_GROUPS):
            st_ref[0, g * HEADS_PER_GROUP:(g + 1) * HEADS_PER_GROUP] = (
                state_ref[g].T.reshape(HEADS_PER_GROUP, SSM_HEAD_DIM, D_STATE))


def _ssd_prompt(xbc, dt, z, a_log, d_skip_e, norm_w, e3):
    q = SSD_CHUNK
    n_chunks = SEQ // q
    bc = SSM_GROUPS * D_STATE
    row = lambda b, c: b * n_chunks + c
    const = lambda b, c: (0, 0)
    return pl.pallas_call(
        functools.partial(_ssd_prompt_kernel, n_chunks=n_chunks),
        grid=(BATCH, n_chunks),
        in_specs=[
            pl.BlockSpec((q, D_INNER), lambda b, c: (row(b, c), 0)),
            pl.BlockSpec((q, bc), lambda b, c: (row(b, c), D_INNER // bc)),
            pl.BlockSpec((q, bc), lambda b, c: (row(b, c), D_INNER // bc + 1)),
            pl.BlockSpec((q, HEAD_LANES), lambda b, c: (row(b, c), 0)),
            pl.BlockSpec((q, D_INNER), lambda b, c: (row(b, c), 0)),
            pl.BlockSpec((1, HEAD_LANES), const),
            pl.BlockSpec((1, D_INNER), const),
            pl.BlockSpec((1, D_INNER), const),
            pl.BlockSpec((3 * HEAD_LANES, D_INNER), const),
        ],
        out_specs=[
            pl.BlockSpec((q, D_INNER), lambda b, c: (row(b, c), 0)),
            pl.BlockSpec((1, SSM_HEADS, SSM_HEAD_DIM, D_STATE), lambda b, c: (b, 0, 0, 0)),
        ],
        out_shape=[
            jax.ShapeDtypeStruct((T_ALL, D_INNER), BF16),
            jax.ShapeDtypeStruct((BATCH, SSM_HEADS, SSM_HEAD_DIM, D_STATE), F32),
        ],
        scratch_shapes=[pltpu.VMEM((SSM_GROUPS, D_STATE, GROUP_WIDTH), F32)],
        compiler_params=_params(("arbitrary", "arbitrary"), 48),
        name="ssd_prompt",
    )(xbc, xbc, xbc, dt, z, a_log, d_skip_e, norm_w, e3)


def _ssd_sample_kernel(xs_ref, b_ref, c_ref, dt_ref, z_ref, h0_ref, alog_ref, dskip_ref,
                       nw_ref, e3_ref, gsum_ref, alias_ref, nb_ref, hout_ref, y_ref, *, seqs):
    del alias_ref
    q = DEC_SEQ
    tri = (lax.broadcasted_iota(jnp.int32, (q, q), 0) >= lax.broadcasted_iota(jnp.int32, (q, q), 1)).astype(F32)
    qi = lax.broadcasted_iota(jnp.int32, (q, HEAD_LANES), 0)
    neg_a = -jnp.exp(alog_ref[...])
    for j in range(seqs):
        rows = slice(j * q, (j + 1) * q)
        dt = dt_ref[rows, :]
        acum = jnp.dot(tri, dt * neg_a, preferred_element_type=F32, precision=lax.Precision.HIGHEST)
        a_last = acum[q - 1:q, :]
        cd = jnp.exp(a_last)
        cb16 = c_ref[rows, :].astype(BF16)
        bb16 = b_ref[rows, :].astype(BF16)
        cf = cb16.astype(F32)
        bf = bb16.astype(F32)
        decays, prods = [], []
        for s in range(q):
            seg = acum - acum[s:s + 1, :]
            decays.append(jnp.exp(jnp.where(qi >= s, seg, -jnp.inf)))
            prods.append(cf * bf[s:s + 1, :])
        decay = jnp.concatenate(decays, axis=0)
        prod = jnp.concatenate(prods, axis=0)
        p_hi = prod.astype(BF16)
        p_lo = (prod - p_hi.astype(F32)).astype(BF16)
        cbx = jnp.dot(jnp.concatenate([p_hi, p_lo], axis=1), gsum_ref[...], preferred_element_type=F32)
        per_head = jnp.concatenate(
            [cbx * decay, dt, jnp.exp(a_last - acum), jnp.exp(acum), jnp.broadcast_to(cd, (SUBLANES, HEAD_LANES))],
            axis=0)
        wide = jnp.dot(_split3(per_head), e3_ref[...], preferred_element_type=F32)
        n0 = q * q
        w_e = wide[0:n0]
        dt_e, dend_e, eac_e = wide[n0:n0 + q], wide[n0 + q:n0 + 2 * q], wide[n0 + 2 * q:n0 + 3 * q]
        x = xs_ref[rows, :]
        xdt = x * dt_e
        y_diag = None
        for s in range(q):
            term = w_e[s * q:(s + 1) * q] * xdt[s:s + 1, :]
            y_diag = term if y_diag is None else y_diag + term
        xw = (xdt * dend_e).astype(BF16)
        for g in range(SSM_GROUPS):
            gs = slice(g * GROUP_WIDTH, (g + 1) * GROUP_WIDTH)
            ns = slice(g * D_STATE, (g + 1) * D_STATE)
            hs = slice(g * HEADS_PER_GROUP, (g + 1) * HEADS_PER_GROUP)
            h_prev = h0_ref[j, hs].reshape(GROUP_WIDTH, D_STATE)
            y_off = lax.dot_general(cb16[:, ns], h_prev.astype(BF16), (((1,), (1,)), ((), ())),
                                    preferred_element_type=F32)
            s_add = lax.dot_general(xw[:, gs], bb16[:, ns], (((0,), (0,)), ((), ())),
                                    preferred_element_type=F32)
            for r in range(HEADS_PER_GROUP):
                h = g * HEADS_PER_GROUP + r
                hout_ref[j, h] = (h0_ref[j, h] * cd[:, h:h + 1]
                                  + s_add[r * SSM_HEAD_DIM:(r + 1) * SSM_HEAD_DIM, :])
            y = (y_diag[:, gs] + y_off * eac_e[:, gs]) + dskip_ref[:, gs] * x[:, gs]
            y_ref[rows, gs] = _gated_group_norm(y, z_ref[rows, gs], nw_ref[:, gs])
    nb_ref[...] = y_ref[...].astype(nb_ref.dtype)


def _ssd_sample(xbc, dt, z, h0, a_log, d_skip_e, norm_w, e3, gsum, nb, *, seqs=2):
    rows = seqs * DEC_SEQ
    r0 = T_PROMPT // rows
    bc = SSM_GROUPS * D_STATE
    return pl.pallas_call(
        functools.partial(_ssd_sample_kernel, seqs=seqs),
        grid=(DEC_BATCH // seqs,),
        in_specs=[
            pl.BlockSpec((rows, D_INNER), lambda i: (r0 + i, 0)),
            pl.BlockSpec((rows, bc), lambda i: (r0 + i, D_INNER // bc)),
            pl.BlockSpec((rows, bc), lambda i: (r0 + i, D_INNER // bc + 1)),
            pl.BlockSpec((rows, HEAD_LANES), lambda i: (r0 + i, 0)),
            pl.BlockSpec((rows, D_INNER), lambda i: (r0 + i, 0)),
            pl.BlockSpec((seqs, SSM_HEADS, SSM_HEAD_DIM, D_STATE), lambda i: (i, 0, 0, 0)),
            pl.BlockSpec((1, HEAD_LANES), lambda i: (0, 0)),
            pl.BlockSpec((1, D_INNER), lambda i: (0, 0)),
            pl.BlockSpec((1, D_INNER), lambda i: (0, 0)),
            pl.BlockSpec((3 * HEAD_LANES, D_INNER), lambda i: (0, 0)),
            pl.BlockSpec((2 * bc, HEAD_LANES), lambda i: (0, 0)),
            pl.BlockSpec(memory_space=pl.ANY),
        ],
        out_specs=[
            pl.BlockSpec((rows, D_INNER), lambda i: (r0 + i, 0)),
            pl.BlockSpec((seqs, SSM_HEADS, SSM_HEAD_DIM, D_STATE), lambda i: (i, 0, 0, 0)),
        ],
        out_shape=[
            jax.ShapeDtypeStruct((T_ALL, D_INNER), BF16),
            jax.ShapeDtypeStruct((DEC_BATCH, SSM_HEADS, SSM_HEAD_DIM, D_STATE), F32),
        ],
        scratch_shapes=[pltpu.VMEM((rows, D_INNER), F32)],
        input_output_aliases={11: 0},
        compiler_params=_params(("arbitrary",), 48),
        name="ssd_sample",
    )(xbc, xbc, xbc, dt, z, h0, a_log, d_skip_e, norm_w, e3, gsum, nb)


def _attn_body(q_ref, k_ref, v_ref, o_ref, *, seqs, tq):
    scale = X_HEAD_DIM ** -0.5
    outs = []
    for j in range(seqs):
        q = q_ref[j * tq:(j + 1) * tq, :]
        k = k_ref[j]
        v = v_ref[j]
        heads = []
        for h in range(X_HEADS):
            hs = slice(h * X_HEAD_DIM, (h + 1) * X_HEAD_DIM)
            s = lax.dot_general(q[:, hs].astype(BF16), k[:, hs].astype(BF16), (((1,), (1,)), ((), ())),
                                preferred_element_type=F32) * scale
            p = _softmax_rows(s)
            heads.append(jnp.dot(p.astype(BF16), v[:, hs].astype(BF16), preferred_element_type=F32))
        outs.append(jnp.concatenate(heads, axis=1))
    o = outs[0] if seqs == 1 else jnp.concatenate(outs, axis=0)
    o_ref[...] = o.astype(o_ref.dtype)


def _attn_prompt_kernel(q_ref, k_ref, v_ref, o_ref, *, seqs, tq):
    _attn_body(q_ref, k_ref, v_ref, o_ref, seqs=seqs, tq=tq)


def _attn_sample_kernel(q_ref, k_ref, v_ref, alias_ref, o_ref, *, seqs):
    del alias_ref
    scale = X_HEAD_DIM ** -0.5
    shape = (X_HEADS * DEC_SEQ, N_MEM * X_HEADS)
    own_head = (lax.broadcasted_iota(jnp.int32, shape, 1) % X_HEADS
                == lax.broadcasted_iota(jnp.int32, shape, 0) // DEC_SEQ)
    outs = []
    for j in range(seqs):
        q = q_ref[j * DEC_SEQ:(j + 1) * DEC_SEQ, :]
        q_rows = jnp.concatenate([q[:, h * X_HEAD_DIM:(h + 1) * X_HEAD_DIM] for h in range(X_HEADS)], axis=0)
        s = lax.dot_general(q_rows.astype(BF16), k_ref[j].astype(BF16), (((1,), (1,)), ((), ())),
                            preferred_element_type=F32) * scale
        p = _softmax_rows(jnp.where(own_head, s, -jnp.inf))
        o_rows = jnp.dot(p.astype(BF16), v_ref[j].astype(BF16), preferred_element_type=F32)
        outs.append(jnp.concatenate([o_rows[h * DEC_SEQ:(h + 1) * DEC_SEQ] for h in range(X_HEADS)], axis=1))
    o_ref[...] = jnp.concatenate(outs, axis=0).astype(o_ref.dtype)


def _attention(qg, k_p, v_p, k_s, v_s, *, tq=512, seqs=8):
    nq = SEQ // tq
    o = pl.pallas_call(
        functools.partial(_attn_prompt_kernel, seqs=1, tq=tq),
        grid=(BATCH, nq),
        in_specs=[
            pl.BlockSpec((tq, D_XATT), lambda b, i: (b * nq + i, 0)),
            pl.BlockSpec((1, N_MEM, D_XATT), lambda b, i: (b, 0, 0)),
            pl.BlockSpec((1, N_MEM, D_XATT), lambda b, i: (b, 0, 0)),
        ],
        out_specs=pl.BlockSpec((tq, D_XATT), lambda b, i: (b * nq + i, 0)),
        out_shape=jax.ShapeDtypeStruct((T_ALL, D_XATT), BF16),
        compiler_params=_params(("arbitrary", "arbitrary"), 32),
        name="attn_prompt",
    )(qg, k_p, v_p)
    rows = seqs * DEC_SEQ
    r0 = T_PROMPT // rows
    return pl.pallas_call(
        functools.partial(_attn_sample_kernel, seqs=seqs),
        grid=(DEC_BATCH // seqs,),
        in_specs=[
            pl.BlockSpec((rows, D_XATT), lambda i: (r0 + i, 0)),
            pl.BlockSpec((seqs, N_MEM * X_HEADS, X_HEAD_DIM), lambda i: (i, 0, 0)),
            pl.BlockSpec((seqs, N_MEM * X_HEADS, X_HEAD_DIM), lambda i: (i, 0, 0)),
            pl.BlockSpec(memory_space=pl.ANY),
        ],
        out_specs=pl.BlockSpec((rows, D_XATT), lambda i: (r0 + i, 0)),
        out_shape=jax.ShapeDtypeStruct((T_ALL, D_XATT), BF16),
        input_output_aliases={3: 0},
        compiler_params=_params(("arbitrary",), 40),
        name="attn_sample",
    )(qg, k_s, v_s, o)


def _merge_kernel(ua_ref, nb_ref, ox_ref, ga_ref, gb_ref, gx_ref, wa_ref, wb_ref, wx_ref, o_ref,
                  wa16_ref, wb16_ref, wx16_ref):
    @pl.when(pl.program_id(1) == 0)
    def _():
        wa16_ref[...] = wa_ref[...].astype(BF16)
        wb16_ref[...] = wb_ref[...].astype(BF16)
        wx16_ref[...] = wx_ref[...].astype(BF16)

    ya = jnp.dot(ua_ref[...], wa16_ref[...], preferred_element_type=F32)
    yb = jnp.dot(nb_ref[...], wb16_ref[...], preferred_element_type=F32)
    yx = jnp.dot(ox_ref[...], wx16_ref[...], preferred_element_type=F32)
    merged = (jax.nn.sigmoid(ga_ref[...]) * ya + jax.nn.sigmoid(gb_ref[...]) * yb) + jax.nn.sigmoid(gx_ref[...]) * yx
    o_ref[...] = merged.astype(o_ref.dtype)


def _merge(u_a, nb, ox, gates, w_a, w_b, w_x, *, tm=512, tn=512):
    g0 = 0
    gstep = D_MODEL // tn
    once = {"pipeline_mode": pl.Buffered(1)}
    return pl.pallas_call(
        _merge_kernel,
        grid=(D_MODEL // tn, T_ALL // tm),
        in_specs=[
            pl.BlockSpec((tm, D_CONV_A), lambda n, i: (i, 0)),
            pl.BlockSpec((tm, D_INNER), lambda n, i: (i, 0)),
            pl.BlockSpec((tm, D_XATT), lambda n, i: (i, 0)),
            pl.BlockSpec((tm, tn), lambda n, i: (i, g0 + n)),
            pl.BlockSpec((tm, tn), lambda n, i: (i, g0 + gstep + n)),
            pl.BlockSpec((tm, tn), lambda n, i: (i, g0 + 2 * gstep + n)),
            pl.BlockSpec((D_CONV_A, tn), lambda n, i: (0, n), **once),
            pl.BlockSpec((D_INNER, tn), lambda n, i: (0, n), **once),
            pl.BlockSpec((D_XATT, tn), lambda n, i: (0, n), **once),
        ],
        out_specs=pl.BlockSpec((tm, tn), lambda n, i: (i, n)),
        out_shape=jax.ShapeDtypeStruct((T_ALL, D_MODEL), BF16),
        scratch_shapes=[pltpu.VMEM((D_CONV_A, tn), BF16), pltpu.VMEM((D_INNER, tn), BF16),
                        pltpu.VMEM((D_XATT, tn), BF16)],
        compiler_params=_params(("arbitrary", "arbitrary"), 52),
        name="merge",
    )(u_a, nb, ox, gates, gates, gates, w_a, w_b, w_x)


def _out_proj_resid_kernel(a_ref, w_ref, xp_ref, xs_ref, g_post_ref, g_pre_ref, x1_ref, hn_ref, wb_ref,
                           *, n_prompt_blocks):
    i = pl.program_id(0)

    @pl.when(i == 0)
    def _():
        wb_ref[...] = w_ref[...].astype(BF16)

    def body(x):
        y = jnp.dot(a_ref[...], wb_ref[...], preferred_element_type=F32)
        x1 = x + _rms_scale(y, g_post_ref[...])
        x1_ref[...] = x1
        hn_ref[...] = _rms_scale(x1, g_pre_ref[...]).astype(hn_ref.dtype)

    @pl.when(i < n_prompt_blocks)
    def _():
        body(xp_ref[...])

    @pl.when(i >= n_prompt_blocks)
    def _():
        body(xs_ref[...])


def _out_proj_resid(a, w, xp, xs, g_post, g_pre, *, tm=256):
    n_p, n_s = xp.shape[0] // tm, xs.shape[0] // tm
    d = xp.shape[1]
    k = a.shape[1]
    return pl.pallas_call(
        functools.partial(_out_proj_resid_kernel, n_prompt_blocks=n_p),
        grid=(n_p + n_s,),
        in_specs=[
            pl.BlockSpec((tm, k), lambda i: (i, 0)),
            pl.BlockSpec((k, d), lambda i: (0, 0), pipeline_mode=pl.Buffered(1)),
            pl.BlockSpec((tm, d), lambda i: (jnp.minimum(i, n_p - 1), 0)),
            pl.BlockSpec((tm, d), lambda i: (jnp.maximum(i - n_p, 0), 0)),
            pl.BlockSpec((1, d), lambda i: (0, 0)),
            pl.BlockSpec((1, d), lambda i: (0, 0)),
        ],
        out_specs=[pl.BlockSpec((tm, d), lambda i: (i, 0)), pl.BlockSpec((tm, d), lambda i: (i, 0))],
        out_shape=[jax.ShapeDtypeStruct((T_ALL, d), F32), jax.ShapeDtypeStruct((T_ALL, d), BF16)],
        scratch_shapes=[pltpu.VMEM((k, d), BF16)],
        compiler_params=_params(("arbitrary",), 52),
        name="out_proj_resid",
    )(a, w, xp, xs, g_post, g_pre)


def _resid_out_kernel(x_ref, y_ref, g_ref, o_ref):
    o_ref[...] = x_ref[...] + _rms_scale(y_ref[...], g_ref[...])


def _resid_out(x1, y, g, *, row0, rows, tm=512):
    d = x1.shape[1]
    b0 = row0 // tm
    return pl.pallas_call(
        _resid_out_kernel,
        grid=(rows // tm,),
        in_specs=[
            pl.BlockSpec((tm, d), lambda i: (b0 + i, 0)),
            pl.BlockSpec((tm, d), lambda i: (b0 + i, 0)),
            pl.BlockSpec((1, d), lambda i: (0, 0)),
        ],
        out_specs=pl.BlockSpec((tm, d), lambda i: (i, 0)),
        out_shape=jax.ShapeDtypeStruct((rows, d), F32),
        compiler_params=_params(("arbitrary",), 40),
        name="resid_out",
    )(x1, y, g)


def _head_expand_matrix():
    h = lax.broadcasted_iota(jnp.int32, (3 * HEAD_LANES, D_INNER), 0) % HEAD_LANES
    ch = lax.broadcasted_iota(jnp.int32, (3 * HEAD_LANES, D_INNER), 1) // SSM_HEAD_DIM
    return (h == ch).astype(BF16)


def _group_sum_matrix():
    rows = 2 * SSM_GROUPS * D_STATE
    g = (lax.broadcasted_iota(jnp.int32, (rows, HEAD_LANES), 0) % (SSM_GROUPS * D_STATE)) // D_STATE
    h = lax.broadcasted_iota(jnp.int32, (rows, HEAD_LANES), 1)
    return ((h // HEADS_PER_GROUP == g) & (h < SSM_HEADS)).astype(BF16)


def _pad_lanes(v, width):
    return jnp.pad(v, ((0, 0), (0, width - v.shape[1])))


def kernel(x_prompt, x_sample, mem_prompt, cache_mem_k, cache_mem_v, state_conv_a, state_conv_b, state_ssm, norm_mix_pre, norm_mix_post, norm_mlp_pre, norm_mlp_post, norm_mem, w_in, conv_a_w, w_out_a, conv_b_w, conv_b_bias, dt_bias, a_log, d_skip, ssm_norm_w, w_out_b, w_mem_kv, w_out_x, w_o, w_ff1, w_ff2):
    assert w_in.shape == (1, D_MODEL, OFF_GATES + 3 * D_MODEL)
    assert x_prompt.shape == (BATCH, SEQ, D_MODEL) and x_sample.shape == (DEC_BATCH, DEC_SEQ, D_MODEL)

    xp = x_prompt.reshape(T_PROMPT, D_MODEL)
    xs = x_sample.reshape(T_SAMPLE, D_MODEL)
    wt = jnp.swapaxes(w_in[0], 0, 1)

    mem_n = _norm_cast(mem_prompt.reshape(BATCH * N_MEM, D_MODEL), norm_mem)
    kv = _matmul(mem_n, w_mem_kv[0], n_out=2 * D_XATT, tm=BATCH * N_MEM, tn=512, vmem_mb=40, name="mem_kv")
    k_p = kv[:, :D_XATT].reshape(BATCH, N_MEM, D_XATT)
    v_p = kv[:, D_XATT:].reshape(BATCH, N_MEM, D_XATT)

    xn = _norm_cast2(xp, xs, norm_mix_pre)
    z = _matmul_nt(xn, wt, row0=OFF_Z, n_out=D_INNER, tm=1024, tn=1024, vmem_mb=48, name="in_proj_z")
    gates = _in_proj_gates(xn, wt)
    q, dt = _in_proj_q_dt(xn, wt, _pad_lanes(dt_bias, HEAD_LANES))

    hist_a = jnp.pad(state_conv_a[0], ((0, 0), (DEC_SEQ - (CONV_A_W - 1), 0), (0, 0)))
    u_a, conv_a_p, conv_a_s = _in_proj_mixer_a(xn, wt, conv_a_w[0], hist_a)

    hist_b = jnp.pad(state_conv_b[0], ((0, 0), (DEC_SEQ - (CONV_B_W - 1), 0), (0, 0)))
    xbc, conv_b_p, conv_b_s = _in_proj_xbc(xn, wt, conv_b_w[0], conv_b_bias, hist_b)
    e3 = _head_expand_matrix()
    gsum = _group_sum_matrix()
    a_log_w = _pad_lanes(a_log, HEAD_LANES)
    d_skip_e = jnp.repeat(d_skip, SSM_HEAD_DIM, axis=1)
    nb, ssm_p = _ssd_prompt(xbc, dt, z, a_log_w, d_skip_e, ssm_norm_w, e3)
    nb, ssm_s = _ssd_sample(xbc, dt, z, state_ssm[0], a_log_w, d_skip_e, ssm_norm_w, e3, gsum, nb)

    k_s = cache_mem_k[0].reshape(DEC_BATCH, N_MEM * X_HEADS, X_HEAD_DIM)
    v_s = cache_mem_v[0].reshape(DEC_BATCH, N_MEM * X_HEADS, X_HEAD_DIM)
    ox = _attention(q, k_p, v_p, k_s, v_s)

    merged = _merge(u_a, nb, ox, gates, w_out_a[0], w_out_b[0], w_out_x[0])
    x1, hn = _out_proj_resid(merged, w_o[0], xp, xs, norm_mix_post, norm_mlp_pre)

    hff = _matmul(hn, w_ff1[0], n_out=D_FF, tm=1024, tn=1024, vmem_mb=48, out_dtype=BF16, epilogue="relu2",
                  name="ff1")
    ff = _matmul(hff, w_ff2[0], n_out=D_MODEL, tm=512, tn=512, vmem_mb=52, single_buffer_w=True, name="ff2")
    y_p = _resid_out(x1, ff, norm_mlp_post, row0=0, rows=T_PROMPT)
    y_s = _resid_out(x1, ff, norm_mlp_post, row0=T_PROMPT, rows=T_SAMPLE)

    return (
        y_p.reshape(BATCH, SEQ, D_MODEL),
        y_s.reshape(DEC_BATCH, DEC_SEQ, D_MODEL),
        k_p.reshape(1, BATCH, N_MEM, X_HEADS, X_HEAD_DIM),
        v_p.reshape(1, BATCH, N_MEM, X_HEADS, X_HEAD_DIM),
        conv_a_p[None],
        conv_b_p[None],
        ssm_p[None],
        conv_a_s[None],
        conv_b_s[None],
        ssm_s[None],
    )
```

```python
import functools

import jax
import jax.numpy as jnp
from jax import lax
from jax.experimental import pallas as pl
from jax.experimental.pallas import tpu as pltpu

F32 = jnp.float32
BF16 = jnp.bfloat16

D_MODEL = 2048
BATCH = 4
SEQ = 2048
DEC_BATCH = 128
DEC_SEQ = 8
N_MEM = 256
X_HEADS = 4
X_HEAD_DIM = 128
D_XATT = X_HEADS * X_HEAD_DIM
D_CONV_A = D_MODEL
CONV_A_W = 3
D_INNER = 2 * D_MODEL
SSM_HEAD_DIM = 64
SSM_HEADS = D_INNER // SSM_HEAD_DIM
SSM_GROUPS = 8
HEADS_PER_GROUP = SSM_HEADS // SSM_GROUPS
GROUP_WIDTH = D_INNER // SSM_GROUPS
D_STATE = 128
CONV_B_W = 4
SSD_CHUNK = 128
D_XBC = D_INNER + 2 * SSM_GROUPS * D_STATE
D_FF = 4 * D_MODEL
EPS = 1e-6

T_PROMPT = BATCH * SEQ
T_SAMPLE = DEC_BATCH * DEC_SEQ
T_ALL = T_PROMPT + T_SAMPLE

OFF_BA = 0
OFF_CA = OFF_BA + D_CONV_A
OFF_HA = OFF_CA + D_CONV_A
OFF_Z = OFF_HA + D_CONV_A
OFF_XBC = OFF_Z + D_INNER
OFF_DT = OFF_XBC + D_XBC
OFF_Q = OFF_DT + SSM_HEADS
OFF_GATES = OFF_Q + D_XATT

SUBLANES = 8
LANES = 128
HEAD_LANES = LANES


def _params(semantics, vmem_mb):
    return pltpu.CompilerParams(dimension_semantics=semantics, vmem_limit_bytes=vmem_mb << 20)


def _rms_scale(x, g):
    return x * lax.rsqrt(jnp.mean(x * x, axis=-1, keepdims=True) + EPS) * g


def _softmax_rows(s):
    e = jnp.exp(s - jnp.max(s, axis=-1, keepdims=True))
    return e / jnp.sum(e, axis=-1, keepdims=True)


def _split3_packed(v):
    hi = v.astype(BF16).astype(F32)
    r1 = v - hi
    mid = r1.astype(BF16).astype(F32)
    lo = (r1 - mid).astype(BF16)
    lane = lax.broadcasted_iota(jnp.int32, v.shape, 1)
    first = jnp.where(lane < SSM_HEADS, hi, pltpu.roll(mid, SSM_HEADS, axis=1)).astype(BF16)
    return jnp.concatenate([first, lo], axis=1)


def _norm_cast2_kernel(xp_ref, xs_ref, g_ref, o_ref, *, n_prompt_blocks):
    i = pl.program_id(0)

    @pl.when(i < n_prompt_blocks)
    def _():
        o_ref[...] = _rms_scale(xp_ref[...], g_ref[...]).astype(o_ref.dtype)

    @pl.when(i >= n_prompt_blocks)
    def _():
        o_ref[...] = _rms_scale(xs_ref[...], g_ref[...]).astype(o_ref.dtype)


def _norm_cast2(xp, xs, g, *, tm=512):
    n_p, n_s = xp.shape[0] // tm, xs.shape[0] // tm
    d = xp.shape[1]
    return pl.pallas_call(
        functools.partial(_norm_cast2_kernel, n_prompt_blocks=n_p),
        grid=(n_p + n_s,),
        in_specs=[
            pl.BlockSpec((tm, d), lambda i: (jnp.minimum(i, n_p - 1), 0)),
            pl.BlockSpec((tm, d), lambda i: (jnp.maximum(i - n_p, 0), 0)),
            pl.BlockSpec((1, d), lambda i: (0, 0)),
        ],
        out_specs=pl.BlockSpec((tm, d), lambda i: (i, 0)),
        out_shape=jax.ShapeDtypeStruct((xp.shape[0] + xs.shape[0], d), BF16),
        compiler_params=_params(("arbitrary",), 32),
        name="norm_cast2",
    )(xp, xs, g)


def _norm_cast_kernel(x_ref, g_ref, o_ref):
    o_ref[...] = _rms_scale(x_ref[...], g_ref[...]).astype(o_ref.dtype)


def _norm_cast(x, g, *, tm=512):
    m, d = x.shape
    return pl.pallas_call(
        _norm_cast_kernel,
        grid=(m // tm,),
        in_specs=[pl.BlockSpec((tm, d), lambda i: (i, 0)), pl.BlockSpec((1, d), lambda i: (0, 0))],
        out_specs=pl.BlockSpec((tm, d), lambda i: (i, 0)),
        out_shape=jax.ShapeDtypeStruct((m, d), BF16),
        compiler_params=_params(("arbitrary",), 32),
        name="norm_cast",
    )(x, g)


def _mm_kernel(a_ref, w_ref, o_ref, wb_ref, *, epilogue):
    @pl.when(pl.program_id(1) == 0)
    def _():
        wb_ref[...] = w_ref[...].astype(BF16)

    acc = jnp.dot(a_ref[...], wb_ref[...], preferred_element_type=F32)
    if epilogue == "relu2":
        acc = jnp.square(jnp.maximum(acc, 0.0))
    o_ref[...] = acc.astype(o_ref.dtype)


def _matmul(a, w, *, n_out, tm, tn, vmem_mb, out_dtype=F32, epilogue=None,
            single_buffer_w=False, name="matmul"):
    m, k = a.shape
    w_kwargs = {"pipeline_mode": pl.Buffered(1)} if single_buffer_w else {}
    return pl.pallas_call(
        functools.partial(_mm_kernel, epilogue=epilogue),
        grid=(n_out // tn, m // tm),
        in_specs=[
            pl.BlockSpec((tm, k), lambda n, i: (i, 0)),
            pl.BlockSpec((k, tn), lambda n, i: (0, n), **w_kwargs),
        ],
        out_specs=pl.BlockSpec((tm, tn), lambda n, i: (i, n)),
        out_shape=jax.ShapeDtypeStruct((m, n_out), out_dtype),
        scratch_shapes=[pltpu.VMEM((k, tn), BF16)],
        compiler_params=_params(("arbitrary", "arbitrary"), vmem_mb),
        name=name,
    )(a, w)


NT_DIMS = (((1,), (1,)), ((), ()))
ROW_UNIT = 64


def _mm_nt_kernel(a_ref, w_ref, o_ref, wb_ref):
    @pl.when(pl.program_id(1) == 0)
    def _():
        wb_ref[...] = w_ref[...].astype(BF16)

    o_ref[...] = lax.dot_general(a_ref[...], wb_ref[...], NT_DIMS, preferred_element_type=F32)


def _matmul_nt(a, wt, *, row0, n_out, tm, tn, vmem_mb, name):
    m, k = a.shape
    assert row0 % tn == 0 and n_out % tn == 0
    return pl.pallas_call(
        _mm_nt_kernel,
        grid=(n_out // tn, m // tm),
        in_specs=[
            pl.BlockSpec((tm, k), lambda n, i: (i, 0)),
            pl.BlockSpec((tn, k), lambda n, i: (row0 // tn + n, 0)),
        ],
        out_specs=pl.BlockSpec((tm, tn), lambda n, i: (i, n)),
        out_shape=jax.ShapeDtypeStruct((m, n_out), F32),
        scratch_shapes=[pltpu.VMEM((tn, k), BF16)],
        compiler_params=_params(("arbitrary", "arbitrary"), vmem_mb),
        name=name,
    )(a, wt)


def _gates_kernel(a_ref, wa_ref, wb_ref, wc_ref, o_ref, w16_ref, *, skip):
    @pl.when(pl.program_id(1) == 0)
    def _():
        w = jnp.concatenate([wa_ref[skip:, :], wb_ref[...], wc_ref[...]], axis=0)
        w16_ref[...] = w.astype(BF16)

    o_ref[...] = lax.dot_general(a_ref[...], w16_ref[...], NT_DIMS, preferred_element_type=F32)


def _in_proj_gates(xn, wt, *, tm=1024, tn=1024):
    m, k = xn.shape
    half = tn // 2
    skip = OFF_GATES % half
    rest = skip
    assert skip % ROW_UNIT == 0 and rest == ROW_UNIT
    a0 = OFF_GATES // half
    return pl.pallas_call(
        functools.partial(_gates_kernel, skip=skip),
        grid=(3 * D_MODEL // tn, m // tm),
        in_specs=[
            pl.BlockSpec((tm, k), lambda n, i: (i, 0)),
            pl.BlockSpec((half, k), lambda n, i: (a0 + 2 * n, 0)),
            pl.BlockSpec((half, k), lambda n, i: (a0 + 2 * n + 1, 0)),
            pl.BlockSpec((ROW_UNIT, k), lambda n, i: ((a0 + 2 * n + 2) * (half // ROW_UNIT), 0)),
        ],
        out_specs=pl.BlockSpec((tm, tn), lambda n, i: (i, n)),
        out_shape=jax.ShapeDtypeStruct((m, 3 * D_MODEL), F32),
        scratch_shapes=[pltpu.VMEM((tn, k), BF16)],
        compiler_params=_params(("arbitrary", "arbitrary"), 48),
        name="in_proj_gates",
    )(xn, wt, wt, wt)


def _q_dt_kernel(a_ref, wa_ref, wb_ref, bias_ref, q_ref, dt_ref, w16_ref):
    @pl.when(pl.program_id(0) == 0)
    def _():
        w = jnp.concatenate([wa_ref[SSM_HEADS:, :], wb_ref[...], wa_ref[0:SSM_HEADS, :],
                             jnp.zeros((HEAD_LANES - SSM_HEADS, wa_ref.shape[1]), F32)], axis=0)
        w16_ref[...] = w.astype(BF16)

    out = lax.dot_general(a_ref[...], w16_ref[...], NT_DIMS, preferred_element_type=F32)
    q_ref[...] = out[:, :D_XATT]
    raw = out[:, D_XATT:] + bias_ref[...]
    dt = jnp.maximum(raw, 0.0) + jnp.log1p(jnp.exp(-jnp.abs(raw)))
    lane = lax.broadcasted_iota(jnp.int32, dt.shape, 1)
    dt_ref[...] = jnp.where(lane < SSM_HEADS, dt, 0.0)


def _in_proj_q_dt(xn, wt, dt_bias, *, tm=1024):
    m, k = xn.shape
    assert OFF_DT % D_XATT == 0 and OFF_Q - OFF_DT == SSM_HEADS == ROW_UNIT
    return pl.pallas_call(
        _q_dt_kernel,
        grid=(m // tm,),
        in_specs=[
            pl.BlockSpec((tm, k), lambda i: (i, 0)),
            pl.BlockSpec((D_XATT, k), lambda i: (OFF_DT // D_XATT, 0)),
            pl.BlockSpec((ROW_UNIT, k), lambda i: ((OFF_DT + D_XATT) // ROW_UNIT, 0)),
            pl.BlockSpec((1, HEAD_LANES), lambda i: (0, 0)),
        ],
        out_specs=[
            pl.BlockSpec((tm, D_XATT), lambda i: (i, 0)),
            pl.BlockSpec((tm, HEAD_LANES), lambda i: (i, 0)),
        ],
        out_shape=[
            jax.ShapeDtypeStruct((m, D_XATT), F32),
            jax.ShapeDtypeStruct((m, HEAD_LANES), F32),
        ],
        scratch_shapes=[pltpu.VMEM((D_XATT + HEAD_LANES, k), BF16)],
        compiler_params=_params(("arbitrary",), 40),
        name="in_proj_q_dt",
    )(xn, wt, wt, dt_bias)


TM_CONV = 1024
TILES_PER_SEQ = SEQ // TM_CONV
SAMPLE_TILE = T_PROMPT // TM_CONV
XBC_SUB_BLOCKS = 4


def _causal_taps_rows(u, ext_ref, w, taps, first_of_seq):
    n = u.shape[0]

    @pl.when(first_of_seq)
    def _():
        ext_ref[0:SUBLANES, :] = jnp.zeros((SUBLANES, u.shape[1]), F32)

    ext_ref[SUBLANES:SUBLANES + n, :] = u
    acc = None
    for j in range(taps):
        d = taps - 1 - j
        term = ext_ref[SUBLANES - d:SUBLANES - d + n, :] * w[j:j + 1, :]
        acc = term if acc is None else acc + term
    return acc


def _keep_tail(ext_ref, n):
    ext_ref[0:SUBLANES, :] = ext_ref[n:n + SUBLANES, :]


def _causal_taps_3d(u, hist, w, taps):
    c = u.shape[-1]
    t = lax.broadcasted_iota(jnp.int32, u.shape, 1)
    acc = None
    for j in range(taps):
        d = taps - 1 - j
        if d == 0:
            term = u
        else:
            term = jnp.where(t >= d, pltpu.roll(u, d, axis=1), pltpu.roll(hist, d, axis=1))
        term = term * w[j:j + 1, :].reshape(1, 1, c)
        acc = term if acc is None else acc + term
    return acc


def _mixer_a_kernel(a_ref, wb_ref, wc_ref, wh_ref, cw_ref, hist_ref, u_ref, stp_ref, sts_ref,
                    w16_ref, ext_ref):
    i = pl.program_id(1)
    tm, tn = u_ref.shape
    k = CONV_A_W - 1

    @pl.when(i == 0)
    def _():
        w16_ref[0] = wb_ref[...].astype(BF16)
        w16_ref[1] = wc_ref[...].astype(BF16)
        w16_ref[2] = wh_ref[...].astype(BF16)

    a = a_ref[...]
    b = lax.dot_general(a, w16_ref[0], NT_DIMS, preferred_element_type=F32)
    ch = (lax.dot_general(a, w16_ref[1], NT_DIMS, preferred_element_type=F32)
          * lax.dot_general(a, w16_ref[2], NT_DIMS, preferred_element_type=F32))

    @pl.when(i < SAMPLE_TILE)
    def _():
        conv = _causal_taps_rows(ch, ext_ref, cw_ref[...], CONV_A_W, i % TILES_PER_SEQ == 0)
        u_ref[...] = (b * conv).astype(u_ref.dtype)
        _keep_tail(ext_ref, tm)

        @pl.when(i % TILES_PER_SEQ == TILES_PER_SEQ - 1)
        def _():
            stp_ref[0] = ext_ref[SUBLANES + tm - k:SUBLANES + tm, :]

    @pl.when(i == SAMPLE_TILE)
    def _():
        ch3 = ch.reshape(DEC_BATCH, DEC_SEQ, tn)
        conv = _causal_taps_3d(ch3, hist_ref[...], cw_ref[...], CONV_A_W)
        u = b.reshape(DEC_BATCH, DEC_SEQ, tn) * conv
        u_ref[...] = u.reshape(tm, tn).astype(u_ref.dtype)
        sts_ref[...] = ch3[:, DEC_SEQ - k:, :]


def _in_proj_mixer_a(xn, wt, conv_w, hist_s, *, tn=512):
    tm = TM_CONV
    kdim = xn.shape[1]
    k = CONV_A_W - 1
    last_seq = BATCH - 1
    return pl.pallas_call(
        _mixer_a_kernel,
        grid=(D_CONV_A // tn, T_ALL // tm),
        in_specs=[
            pl.BlockSpec((tm, kdim), lambda n, i: (i, 0)),
            pl.BlockSpec((tn, kdim), lambda n, i: (OFF_BA // tn + n, 0), pipeline_mode=pl.Buffered(1)),
            pl.BlockSpec((tn, kdim), lambda n, i: (OFF_CA // tn + n, 0), pipeline_mode=pl.Buffered(1)),
            pl.BlockSpec((tn, kdim), lambda n, i: (OFF_HA // tn + n, 0), pipeline_mode=pl.Buffered(1)),
            pl.BlockSpec((CONV_A_W, tn), lambda n, i: (0, n)),
            pl.BlockSpec((DEC_BATCH, DEC_SEQ, tn), lambda n, i: (0, 0, n)),
        ],
        out_specs=[
            pl.BlockSpec((tm, tn), lambda n, i: (i, n)),
            pl.BlockSpec((1, k, tn), lambda n, i: (jnp.minimum(i // TILES_PER_SEQ, last_seq), 0, n)),
            pl.BlockSpec((DEC_BATCH, k, tn), lambda n, i: (0, 0, n)),
        ],
        out_shape=[
            jax.ShapeDtypeStruct((T_ALL, D_CONV_A), BF16),
            jax.ShapeDtypeStruct((BATCH, k, D_CONV_A), F32),
            jax.ShapeDtypeStruct((DEC_BATCH, k, D_CONV_A), F32),
        ],
        scratch_shapes=[pltpu.VMEM((3, tn, kdim), BF16), pltpu.VMEM((SUBLANES + tm, tn), F32)],
        compiler_params=_params(("arbitrary", "arbitrary"), 56),
        name="in_proj_mixer_a",
    )(xn, wt, wt, wt, conv_w, hist_s)


def _xbc_kernel(a_ref, w_ref, cw_ref, cbias_ref, hist_ref, o_ref, stp_ref, sts_ref, w16_ref, ext_ref):
    i = pl.program_id(1)
    tm, tn = o_ref.shape
    k = CONV_B_W - 1

    @pl.when(i == 0)
    def _():
        w16_ref[...] = w_ref[...].astype(BF16)

    @pl.when(i < SAMPLE_TILE)
    def _():
        @pl.when(i % TILES_PER_SEQ == 0)
        def _():
            ext_ref[0, 0:SUBLANES, :] = jnp.zeros((SUBLANES, tn), F32)

        sub = tm // XBC_SUB_BLOCKS
        cw = cw_ref[...]
        bias = cbias_ref[...]

        def project(s):
            ext_ref[s % 2, SUBLANES:SUBLANES + sub, :] = lax.dot_general(
                a_ref[s * sub:(s + 1) * sub, :], w16_ref[...], NT_DIMS, preferred_element_type=F32)

        project(0)
        for s in range(XBC_SUB_BLOCKS):
            if s + 1 < XBC_SUB_BLOCKS:
                project(s + 1)
            acc = None
            for j in range(CONV_B_W):
                d = CONV_B_W - 1 - j
                term = ext_ref[s % 2, SUBLANES - d:SUBLANES - d + sub, :] * cw[j:j + 1, :]
                acc = term if acc is None else acc + term
            conv = acc + bias
            o_ref[s * sub:(s + 1) * sub, :] = conv * jax.nn.sigmoid(conv)
            ext_ref[(s + 1) % 2, 0:SUBLANES, :] = ext_ref[s % 2, sub:sub + SUBLANES, :]

        @pl.when(i % TILES_PER_SEQ == TILES_PER_SEQ - 1)
        def _():
            last = (XBC_SUB_BLOCKS - 1) % 2
            stp_ref[0] = ext_ref[last, SUBLANES + sub - k:SUBLANES + sub, :]

    @pl.when(i == SAMPLE_TILE)
    def _():
        raw = lax.dot_general(a_ref[...], w16_ref[...], NT_DIMS, preferred_element_type=F32)
        raw3 = raw.reshape(DEC_BATCH, DEC_SEQ, tn)
        conv = _causal_taps_3d(raw3, hist_ref[...], cw_ref[...], CONV_B_W) + cbias_ref[...].reshape(1, 1, tn)
        o_ref[...] = (conv * jax.nn.sigmoid(conv)).reshape(tm, tn)
        sts_ref[...] = raw3[:, DEC_SEQ - k:, :]


def _in_proj_xbc(xn, wt, conv_w, conv_bias, hist_s, *, tn=512):
    tm = TM_CONV
    kdim = xn.shape[1]
    k = CONV_B_W - 1
    last_seq = BATCH - 1
    return pl.pallas_call(
        _xbc_kernel,
        grid=(D_XBC // tn, T_ALL // tm),
        in_specs=[
            pl.BlockSpec((tm, kdim), lambda n, i: (i, 0)),
            pl.BlockSpec((tn, kdim), lambda n, i: (OFF_XBC // tn + n, 0)),
            pl.BlockSpec((CONV_B_W, tn), lambda n, i: (0, n)),
            pl.BlockSpec((1, tn), lambda n, i: (0, n)),
            pl.BlockSpec((DEC_BATCH, DEC_SEQ, tn), lambda n, i: (0, 0, n)),
        ],
        out_specs=[
            pl.BlockSpec((tm, tn), lambda n, i: (i, n)),
            pl.BlockSpec((1, k, tn), lambda n, i: (jnp.minimum(i // TILES_PER_SEQ, last_seq), 0, n)),
            pl.BlockSpec((DEC_BATCH, k, tn), lambda n, i: (0, 0, n)),
        ],
        out_shape=[
            jax.ShapeDtypeStruct((T_ALL, D_XBC), F32),
            jax.ShapeDtypeStruct((BATCH, k, D_XBC), F32),
            jax.ShapeDtypeStruct((DEC_BATCH, k, D_XBC), F32),
        ],
        scratch_shapes=[pltpu.VMEM((tn, kdim), BF16),
                        pltpu.VMEM((2, SUBLANES + tm // XBC_SUB_BLOCKS, tn), F32)],
        compiler_params=_params(("arbitrary", "arbitrary"), 48),
        name="in_proj_xbc",
    )(xn, wt, conv_w, conv_bias, hist_s)


def _gated_group_norm(y, z, w):
    u = y * (z * jax.nn.sigmoid(z))
    return u * lax.rsqrt(jnp.mean(u * u, axis=-1, keepdims=True) + EPS) * w


def _ssd_prompt_kernel(xs_ref, b_ref, c_ref, dt_ref, z_ref, alog_ref, dskip_ref,
                       nw_ref, e_ref, nb_ref, st_ref, state_ref, *, n_chunks):
    q = SSD_CHUNK
    ci = pl.program_id(1)

    @pl.when(ci == 0)
    def _():
        state_ref[...] = jnp.zeros_like(state_ref)

    row = lax.broadcasted_iota(jnp.int32, (q, q), 0)
    col = lax.broadcasted_iota(jnp.int32, (q, q), 1)
    causal = row >= col
    dt = dt_ref[...]
    da = dt * (-jnp.exp(alog_ref[...]))
    acum = jnp.dot(causal.astype(F32), da, preferred_element_type=F32,
                   precision=lax.Precision.HIGHEST)
    acum_t = acum.T
    dt_t = dt.T
    a_last = acum[q - 1:q, :]
    per_head = jnp.concatenate(
        [dt * jnp.exp(a_last - acum), jnp.exp(acum), jnp.broadcast_to(jnp.exp(a_last), (SUBLANES, HEAD_LANES))],
        axis=0)
    wide = jnp.dot(_split3_packed(per_head), e_ref[...], preferred_element_type=F32)
    w_e, eac_e, cd_e = wide[0:q], wide[q:2 * q], wide[2 * q:2 * q + 1]

    lane = lax.broadcasted_iota(jnp.int32, (q, LANES), 1)
    first_head = lane < SSM_HEAD_DIM
    for g in range(SSM_GROUPS):
        gs = slice(g * GROUP_WIDTH, (g + 1) * GROUP_WIDTH)
        ns = slice(g * D_STATE, (g + 1) * D_STATE)
        xg = xs_ref[:, gs]
        bg = b_ref[:, ns].astype(BF16)
        cg = c_ref[:, ns].astype(BF16)
        cb = lax.dot_general(cg, bg, (((1,), (1,)), ((), ())), preferred_element_type=F32)
        s_prev = state_ref[g]
        y_off = jnp.dot(cg, s_prev.astype(BF16), preferred_element_type=F32)
        xw = (xg * w_e[:, gs]).astype(BF16)
        s_add = lax.dot_general(bg, xw, (((0,), (0,)), ((), ())), preferred_element_type=F32)
        state_ref[g] = s_prev * cd_e[:, gs] + s_add
        pairs = []
        for k in range(HEADS_PER_GROUP // 2):
            h0 = g * HEADS_PER_GROUP + 2 * k
            ms = []
            for h in (h0, h0 + 1):
                seg = acum[:, h:h + 1] - acum_t[h:h + 1, :]
                decay = jnp.exp(jnp.where(causal, seg, -jnp.inf))
                ms.append((cb * decay * dt_t[h:h + 1, :]).astype(BF16))
            lhs = jnp.concatenate(ms, axis=1)
            xp = xg[:, k * LANES:(k + 1) * LANES]
            rhs = jnp.concatenate([jnp.where(first_head, xp, 0.0).astype(BF16),
                                   jnp.where(first_head, 0.0, xp).astype(BF16)], axis=0)
            pairs.append(jnp.dot(lhs, rhs, preferred_element_type=F32))
        y_diag = jnp.concatenate(pairs, axis=1)
        y = (y_diag + y_off * eac_e[:, gs]) + dskip_ref[:, gs] * xg
        nb_ref[:, gs] = _gated_group_norm(y, z_ref[:, gs], nw_ref[:, gs]).astype(nb_ref.dtype)

    @pl.when(ci == n_chunks - 1)
    def _():
        for g in range(SSM_GROUPS):
            st_ref[0, g * HEADS_PER_GROUP:(g + 1) * HEADS_PER_GROUP] = (
                state_ref[g].T.reshape(HEADS_PER_GROUP, SSM_HEAD_DIM, D_STATE))


def _ssd_prompt(xbc, dt, z, a_log, d_skip_e, norm_w, e3):
    q = SSD_CHUNK
    n_chunks = SEQ // q
    bc = SSM_GROUPS * D_STATE
    row = lambda b, c: b * n_chunks + c
    const = lambda b, c: (0, 0)
    return pl.pallas_call(
        functools.partial(_ssd_prompt_kernel, n_chunks=n_chunks),
        grid=(BATCH, n_chunks),
        in_specs=[
            pl.BlockSpec((q, D_INNER), lambda b, c: (row(b, c), 0)),
            pl.BlockSpec((q, bc), lambda b, c: (row(b, c), D_INNER // bc)),
            pl.BlockSpec((q, bc), lambda b, c: (row(b, c), D_INNER // bc + 1)),
            pl.BlockSpec((q, HEAD_LANES), lambda b, c: (row(b, c), 0)),
            pl.BlockSpec((q, D_INNER), lambda b, c: (row(b, c), 0)),
            pl.BlockSpec((1, HEAD_LANES), const),
            pl.BlockSpec((1, D_INNER), const),
            pl.BlockSpec((1, D_INNER), const),
            pl.BlockSpec((2 * HEAD_LANES, D_INNER), const),
        ],
        out_specs=[
            pl.BlockSpec((q, D_INNER), lambda b, c: (row(b, c), 0)),
            pl.BlockSpec((1, SSM_HEADS, SSM_HEAD_DIM, D_STATE), lambda b, c: (b, 0, 0, 0)),
        ],
        out_shape=[
            jax.ShapeDtypeStruct((T_ALL, D_INNER), BF16),
            jax.ShapeDtypeStruct((BATCH, SSM_HEADS, SSM_HEAD_DIM, D_STATE), F32),
        ],
        scratch_shapes=[pltpu.VMEM((SSM_GROUPS, D_STATE, GROUP_WIDTH), F32)],
        compiler_params=_params(("arbitrary", "arbitrary"), 48),
        name="ssd_prompt",
    )(xbc, xbc, xbc, dt, z, a_log, d_skip_e, norm_w, e3)


def _ssd_sample_kernel(xs_ref, b_ref, c_ref, dt_ref, z_ref, h0_ref, alog_ref, dskip_ref,
                       nw_ref, e_ref, gsum_ref, alias_ref, nb_ref, hout_ref,
                       y_ref, ph_ref, prod_ref, wide_ref, *, seqs):
    del alias_ref
    q = DEC_SEQ
    n0 = q * q
    per = n0 + 3 * q
    tri = (lax.broadcasted_iota(jnp.int32, (q, q), 0) >= lax.broadcasted_iota(jnp.int32, (q, q), 1)).astype(F32)
    qi = lax.broadcasted_iota(jnp.int32, (q, HEAD_LANES), 0)
    neg_a = -jnp.exp(alog_ref[...])
    for j in range(seqs):
        rows = slice(j * q, (j + 1) * q)
        base = j * per
        dt = dt_ref[rows, :]
        acum = jnp.dot(tri, dt * neg_a, preferred_element_type=F32, precision=lax.Precision.HIGHEST)
        a_last = acum[q - 1:q, :]
        ph_ref[base + n0:base + n0 + q, :] = dt * jnp.exp(a_last - acum)
        ph_ref[base + n0 + q:base + n0 + 2 * q, :] = jnp.exp(acum)
        ph_ref[base + n0 + 2 * q:base + per, :] = jnp.broadcast_to(jnp.exp(a_last), (q, HEAD_LANES))
        cf = c_ref[rows, :].astype(BF16).astype(F32)
        bf = b_ref[rows, :].astype(BF16).astype(F32)
        for s in range(q):
            seg = acum - acum[s:s + 1, :]
            ph_ref[base + s * q:base + (s + 1) * q, :] = (
                jnp.exp(jnp.where(qi >= s, seg, -jnp.inf)) * dt[s:s + 1, :])
            prod_ref[j * n0 + s * q:j * n0 + (s + 1) * q, :] = cf * bf[s:s + 1, :]
    prod = prod_ref[...]
    p_hi = prod.astype(BF16)
    p_lo = (prod - p_hi.astype(F32)).astype(BF16)
    cbx = jnp.dot(jnp.concatenate([p_hi, p_lo], axis=1), gsum_ref[...], preferred_element_type=F32)
    for j in range(seqs):
        ph_ref[j * per:j * per + n0, :] = ph_ref[j * per:j * per + n0, :] * cbx[j * n0:(j + 1) * n0, :]
    wide_ref[...] = jnp.dot(_split3_packed(ph_ref[...]), e_ref[...], preferred_element_type=F32)

    for j in range(seqs):
        rows = slice(j * q, (j + 1) * q)
        base = j * per
        x = xs_ref[rows, :]
        y_diag = None
        for s in range(q):
            term = wide_ref[base + s * q:base + (s + 1) * q, :] * x[s:s + 1, :]
            y_diag = term if y_diag is None else y_diag + term
        eac_e = wide_ref[base + n0 + q:base + n0 + 2 * q, :]
        cd = ph_ref[base + n0 + 2 * q:base + n0 + 2 * q + 1, :]
        xw = (x * wide_ref[base + n0:base + n0 + q, :]).astype(BF16)
        cb16 = c_ref[rows, :].astype(BF16)
        bb16 = b_ref[rows, :].astype(BF16)
        for g in range(SSM_GROUPS):
            gs = slice(g * GROUP_WIDTH, (g + 1) * GROUP_WIDTH)
            ns = slice(g * D_STATE, (g + 1) * D_STATE)
            hs = slice(g * HEADS_PER_GROUP, (g + 1) * HEADS_PER_GROUP)
            h_prev = h0_ref[j, hs].reshape(GROUP_WIDTH, D_STATE)
            y_off = lax.dot_general(cb16[:, ns], h_prev.astype(BF16), (((1,), (1,)), ((), ())),
                                    preferred_element_type=F32)
            s_add = lax.dot_general(xw[:, gs], bb16[:, ns], (((0,), (0,)), ((), ())),
                                    preferred_element_type=F32)
            for r in range(HEADS_PER_GROUP):
                h = g * HEADS_PER_GROUP + r
                hout_ref[j, h] = (h0_ref[j, h] * cd[:, h:h + 1]
                                  + s_add[r * SSM_HEAD_DIM:(r + 1) * SSM_HEAD_DIM, :])
            y = (y_diag[:, gs] + y_off * eac_e[:, gs]) + dskip_ref[:, gs] * x[:, gs]
            y_ref[rows, gs] = _gated_group_norm(y, z_ref[rows, gs], nw_ref[:, gs])
    nb_ref[...] = y_ref[...].astype(nb_ref.dtype)


def _ssd_sample(xbc, dt, z, h0, a_log, d_skip_e, norm_w, e3, gsum, nb, *, seqs=4):
    rows = seqs * DEC_SEQ
    r0 = T_PROMPT // rows
    bc = SSM_GROUPS * D_STATE
    return pl.pallas_call(
        functools.partial(_ssd_sample_kernel, seqs=seqs),
        grid=(DEC_BATCH // seqs,),
        in_specs=[
            pl.BlockSpec((rows, D_INNER), lambda i: (r0 + i, 0)),
            pl.BlockSpec((rows, bc), lambda i: (r0 + i, D_INNER // bc)),
            pl.BlockSpec((rows, bc), lambda i: (r0 + i, D_INNER // bc + 1)),
            pl.BlockSpec((rows, HEAD_LANES), lambda i: (r0 + i, 0)),
            pl.BlockSpec((rows, D_INNER), lambda i: (r0 + i, 0)),
            pl.BlockSpec((seqs, SSM_HEADS, SSM_HEAD_DIM, D_STATE), lambda i: (i, 0, 0, 0)),
            pl.BlockSpec((1, HEAD_LANES), lambda i: (0, 0)),
            pl.BlockSpec((1, D_INNER), lambda i: (0, 0)),
            pl.BlockSpec((1, D_INNER), lambda i: (0, 0)),
            pl.BlockSpec((2 * HEAD_LANES, D_INNER), lambda i: (0, 0)),
            pl.BlockSpec((2 * bc, HEAD_LANES), lambda i: (0, 0)),
            pl.BlockSpec(memory_space=pl.ANY),
        ],
        out_specs=[
            pl.BlockSpec((rows, D_INNER), lambda i: (r0 + i, 0)),
            pl.BlockSpec((seqs, SSM_HEADS, SSM_HEAD_DIM, D_STATE), lambda i: (i, 0, 0, 0)),
        ],
        out_shape=[
            jax.ShapeDtypeStruct((T_ALL, D_INNER), BF16),
            jax.ShapeDtypeStruct((DEC_BATCH, SSM_HEADS, SSM_HEAD_DIM, D_STATE), F32),
        ],
        scratch_shapes=[
            pltpu.VMEM((rows, D_INNER), F32),
            pltpu.VMEM((seqs * (DEC_SEQ + 3) * DEC_SEQ, HEAD_LANES), F32),
            pltpu.VMEM((seqs * DEC_SEQ * DEC_SEQ, bc), F32),
            pltpu.VMEM((seqs * (DEC_SEQ + 3) * DEC_SEQ, D_INNER), F32),
        ],
        input_output_aliases={11: 0},
        compiler_params=_params(("arbitrary",), 56),
        name="ssd_sample",
    )(xbc, xbc, xbc, dt, z, h0, a_log, d_skip_e, norm_w, e3, gsum, nb)


def _attn_body(q_ref, k_ref, v_ref, o_ref, *, seqs, tq):
    scale = X_HEAD_DIM ** -0.5
    outs = []
    for j in range(seqs):
        q = q_ref[j * tq:(j + 1) * tq, :]
        k = k_ref[j]
        v = v_ref[j]
        heads = []
        for h in range(X_HEADS):
            hs = slice(h * X_HEAD_DIM, (h + 1) * X_HEAD_DIM)
            s = lax.dot_general(q[:, hs].astype(BF16), k[:, hs].astype(BF16), (((1,), (1,)), ((), ())),
                                preferred_element_type=F32) * scale
            p = _softmax_rows(s)
            heads.append(jnp.dot(p.astype(BF16), v[:, hs].astype(BF16), preferred_element_type=F32))
        outs.append(jnp.concatenate(heads, axis=1))
    o = outs[0] if seqs == 1 else jnp.concatenate(outs, axis=0)
    o_ref[...] = o.astype(o_ref.dtype)


def _attn_prompt_kernel(q_ref, k_ref, v_ref, o_ref, *, seqs, tq):
    _attn_body(q_ref, k_ref, v_ref, o_ref, seqs=seqs, tq=tq)


def _attn_sample_kernel(q_ref, k_ref, v_ref, alias_ref, o_ref, *, seqs):
    del alias_ref
    scale = X_HEAD_DIM ** -0.5
    shape = (X_HEADS * DEC_SEQ, N_MEM * X_HEADS)
    own_head = (lax.broadcasted_iota(jnp.int32, shape, 1) % X_HEADS
                == lax.broadcasted_iota(jnp.int32, shape, 0) // DEC_SEQ)
    outs = []
    for j in range(seqs):
        q = q_ref[j * DEC_SEQ:(j + 1) * DEC_SEQ, :]
        q_rows = jnp.concatenate([q[:, h * X_HEAD_DIM:(h + 1) * X_HEAD_DIM] for h in range(X_HEADS)], axis=0)
        s = lax.dot_general(q_rows.astype(BF16), k_ref[j].astype(BF16), (((1,), (1,)), ((), ())),
                            preferred_element_type=F32) * scale
        p = _softmax_rows(jnp.where(own_head, s, -jnp.inf))
        o_rows = jnp.dot(p.astype(BF16), v_ref[j].astype(BF16), preferred_element_type=F32)
        outs.append(jnp.concatenate([o_rows[h * DEC_SEQ:(h + 1) * DEC_SEQ] for h in range(X_HEADS)], axis=1))
    o_ref[...] = jnp.concatenate(outs, axis=0).astype(o_ref.dtype)


def _attention(qg, k_p, v_p, k_s, v_s, *, tq=512, seqs=8):
    nq = SEQ // tq
    o = pl.pallas_call(
        functools.partial(_attn_prompt_kernel, seqs=1, tq=tq),
        grid=(BATCH, nq),
        in_specs=[
            pl.BlockSpec((tq, D_XATT), lambda b, i: (b * nq + i, 0)),
            pl.BlockSpec((1, N_MEM, D_XATT), lambda b, i: (b, 0, 0)),
            pl.BlockSpec((1, N_MEM, D_XATT), lambda b, i: (b, 0, 0)),
        ],
        out_specs=pl.BlockSpec((tq, D_XATT), lambda b, i: (b * nq + i, 0)),
        out_shape=jax.ShapeDtypeStruct((T_ALL, D_XATT), BF16),
        compiler_params=_params(("arbitrary", "arbitrary"), 32),
        name="attn_prompt",
    )(qg, k_p, v_p)
    rows = seqs * DEC_SEQ
    r0 = T_PROMPT // rows
    return pl.pallas_call(
        functools.partial(_attn_sample_kernel, seqs=seqs),
        grid=(DEC_BATCH // seqs,),
        in_specs=[
            pl.BlockSpec((rows, D_XATT), lambda i: (r0 + i, 0)),
            pl.BlockSpec((seqs, N_MEM * X_HEADS, X_HEAD_DIM), lambda i: (i, 0, 0)),
            pl.BlockSpec((seqs, N_MEM * X_HEADS, X_HEAD_DIM), lambda i: (i, 0, 0)),
            pl.BlockSpec(memory_space=pl.ANY),
        ],
        out_specs=pl.BlockSpec((rows, D_XATT), lambda i: (r0 + i, 0)),
        out_shape=jax.ShapeDtypeStruct((T_ALL, D_XATT), BF16),
        input_output_aliases={3: 0},
        compiler_params=_params(("arbitrary",), 40),
        name="attn_sample",
    )(qg, k_s, v_s, o)


def _merge_kernel(ua_ref, nb_ref, ox_ref, ga_ref, gb_ref, gx_ref, wa_ref, wb_ref, wx_ref, o_ref,
                  wa16_ref, wb16_ref, wx16_ref):
    @pl.when(pl.program_id(1) == 0)
    def _():
        wa16_ref[...] = wa_ref[...].astype(BF16)
        wb16_ref[...] = wb_ref[...].astype(BF16)
        wx16_ref[...] = wx_ref[...].astype(BF16)

    ya = jnp.dot(ua_ref[...], wa16_ref[...], preferred_element_type=F32)
    yb = jnp.dot(nb_ref[...], wb16_ref[...], preferred_element_type=F32)
    yx = jnp.dot(ox_ref[...], wx16_ref[...], preferred_element_type=F32)
    merged = (jax.nn.sigmoid(ga_ref[...]) * ya + jax.nn.sigmoid(gb_ref[...]) * yb) + jax.nn.sigmoid(gx_ref[...]) * yx
    o_ref[...] = merged.astype(o_ref.dtype)


def _merge(u_a, nb, ox, gates, w_a, w_b, w_x, *, tm=512, tn=512):
    g0 = 0
    gstep = D_MODEL // tn
    once = {"pipeline_mode": pl.Buffered(1)}
    return pl.pallas_call(
        _merge_kernel,
        grid=(D_MODEL // tn, T_ALL // tm),
        in_specs=[
            pl.BlockSpec((tm, D_CONV_A), lambda n, i: (i, 0)),
            pl.BlockSpec((tm, D_INNER), lambda n, i: (i, 0)),
            pl.BlockSpec((tm, D_XATT), lambda n, i: (i, 0)),
            pl.BlockSpec((tm, tn), lambda n, i: (i, g0 + n)),
            pl.BlockSpec((tm, tn), lambda n, i: (i, g0 + gstep + n)),
            pl.BlockSpec((tm, tn), lambda n, i: (i, g0 + 2 * gstep + n)),
            pl.BlockSpec((D_CONV_A, tn), lambda n, i: (0, n), **once),
            pl.BlockSpec((D_INNER, tn), lambda n, i: (0, n), **once),
            pl.BlockSpec((D_XATT, tn), lambda n, i: (0, n), **once),
        ],
        out_specs=pl.BlockSpec((tm, tn), lambda n, i: (i, n)),
        out_shape=jax.ShapeDtypeStruct((T_ALL, D_MODEL), BF16),
        scratch_shapes=[pltpu.VMEM((D_CONV_A, tn), BF16), pltpu.VMEM((D_INNER, tn), BF16),
                        pltpu.VMEM((D_XATT, tn), BF16)],
        compiler_params=_params(("arbitrary", "arbitrary"), 52),
        name="merge",
    )(u_a, nb, ox, gates, gates, gates, w_a, w_b, w_x)


def _out_proj_resid_kernel(a_ref, w_ref, xp_ref, xs_ref, g_post_ref, g_pre_ref, x1_ref, hn_ref, wb_ref,
                           *, n_prompt_blocks):
    i = pl.program_id(0)

    @pl.when(i == 0)
    def _():
        wb_ref[...] = w_ref[...].astype(BF16)

    def body(x):
        y = jnp.dot(a_ref[...], wb_ref[...], preferred_element_type=F32)
        x1 = x + _rms_scale(y, g_post_ref[...])
        x1_ref[...] = x1
        hn_ref[...] = _rms_scale(x1, g_pre_ref[...]).astype(hn_ref.dtype)

    @pl.when(i < n_prompt_blocks)
    def _():
        body(xp_ref[...])

    @pl.when(i >= n_prompt_blocks)
    def _():
        body(xs_ref[...])


def _out_proj_resid(a, w, xp, xs, g_post, g_pre, *, tm=256):
    n_p, n_s = xp.shape[0] // tm, xs.shape[0] // tm
    d = xp.shape[1]
    k = a.shape[1]
    return pl.pallas_call(
        functools.partial(_out_proj_resid_kernel, n_prompt_blocks=n_p),
        grid=(n_p + n_s,),
        in_specs=[
            pl.BlockSpec((tm, k), lambda i: (i, 0)),
            pl.BlockSpec((k, d), lambda i: (0, 0), pipeline_mode=pl.Buffered(1)),
            pl.BlockSpec((tm, d), lambda i: (jnp.minimum(i, n_p - 1), 0)),
            pl.BlockSpec((tm, d), lambda i: (jnp.maximum(i - n_p, 0), 0)),
            pl.BlockSpec((1, d), lambda i: (0, 0)),
            pl.BlockSpec((1, d), lambda i: (0, 0)),
        ],
        out_specs=[pl.BlockSpec((tm, d), lambda i: (i, 0)), pl.BlockSpec((tm, d), lambda i: (i, 0))],
        out_shape=[jax.ShapeDtypeStruct((T_ALL, d), F32), jax.ShapeDtypeStruct((T_ALL, d), BF16)],
        scratch_shapes=[pltpu.VMEM((k, d), BF16)],
        compiler_params=_params(("arbitrary",), 52),
        name="out_proj_resid",
    )(a, w, xp, xs, g_post, g_pre)


def _resid_out_kernel(x_ref, y_ref, g_ref, o_ref):
    o_ref[...] = x_ref[...] + _rms_scale(y_ref[...], g_ref[...])


def _resid_out(x1, y, g, *, row0, rows, tm=512):
    d = x1.shape[1]
    b0 = row0 // tm
    return pl.pallas_call(
        _resid_out_kernel,
        grid=(rows // tm,),
        in_specs=[
            pl.BlockSpec((tm, d), lambda i: (b0 + i, 0)),
            pl.BlockSpec((tm, d), lambda i: (b0 + i, 0)),
            pl.BlockSpec((1, d), lambda i: (0, 0)),
        ],
        out_specs=pl.BlockSpec((tm, d), lambda i: (i, 0)),
        out_shape=jax.ShapeDtypeStruct((rows, d), F32),
        compiler_params=_params(("arbitrary",), 40),
        name="resid_out",
    )(x1, y, g)


def _head_expand_matrix():
    r = lax.broadcasted_iota(jnp.int32, (2 * HEAD_LANES, D_INNER), 0)
    h = jnp.where(r < 3 * SSM_HEADS, r % SSM_HEADS, -1)
    ch = lax.broadcasted_iota(jnp.int32, (2 * HEAD_LANES, D_INNER), 1) // SSM_HEAD_DIM
    return (h == ch).astype(BF16)


def _group_sum_matrix():
    rows = 2 * SSM_GROUPS * D_STATE
    g = (lax.broadcasted_iota(jnp.int32, (rows, HEAD_LANES), 0) % (SSM_GROUPS * D_STATE)) // D_STATE
    h = lax.broadcasted_iota(jnp.int32, (rows, HEAD_LANES), 1)
    return ((h // HEADS_PER_GROUP == g) & (h < SSM_HEADS)).astype(BF16)


def _pad_lanes(v, width):
    return jnp.pad(v, ((0, 0), (0, width - v.shape[1])))


def kernel(x_prompt, x_sample, mem_prompt, cache_mem_k, cache_mem_v, state_conv_a, state_conv_b, state_ssm, norm_mix_pre, norm_mix_post, norm_mlp_pre, norm_mlp_post, norm_mem, w_in, conv_a_w, w_out_a, conv_b_w, conv_b_bias, dt_bias, a_log, d_skip, ssm_norm_w, w_out_b, w_mem_kv, w_out_x, w_o, w_ff1, w_ff2):
    assert w_in.shape == (1, D_MODEL, OFF_GATES + 3 * D_MODEL)
    assert x_prompt.shape == (BATCH, SEQ, D_MODEL) and x_sample.shape == (DEC_BATCH, DEC_SEQ, D_MODEL)

    xp = x_prompt.reshape(T_PROMPT, D_MODEL)
    xs = x_sample.reshape(T_SAMPLE, D_MODEL)
    wt = jnp.swapaxes(w_in[0], 0, 1)

    mem_n = _norm_cast(mem_prompt.reshape(BATCH * N_MEM, D_MODEL), norm_mem)
    kv = _matmul(mem_n, w_mem_kv[0], n_out=2 * D_XATT, tm=BATCH * N_MEM, tn=512, vmem_mb=40, name="mem_kv")
    k_p = kv[:, :D_XATT].reshape(BATCH, N_MEM, D_XATT)
    v_p = kv[:, D_XATT:].reshape(BATCH, N_MEM, D_XATT)

    xn = _norm_cast2(xp, xs, norm_mix_pre)
    z = _matmul_nt(xn, wt, row0=OFF_Z, n_out=D_INNER, tm=1024, tn=1024, vmem_mb=48, name="in_proj_z")
    gates = _in_proj_gates(xn, wt)
    q, dt = _in_proj_q_dt(xn, wt, _pad_lanes(dt_bias, HEAD_LANES))

    hist_a = jnp.pad(state_conv_a[0], ((0, 0), (DEC_SEQ - (CONV_A_W - 1), 0), (0, 0)))
    u_a, conv_a_p, conv_a_s = _in_proj_mixer_a(xn, wt, conv_a_w[0], hist_a)

    hist_b = jnp.pad(state_conv_b[0], ((0, 0), (DEC_SEQ - (CONV_B_W - 1), 0), (0, 0)))
    xbc, conv_b_p, conv_b_s = _in_proj_xbc(xn, wt, conv_b_w[0], conv_b_bias, hist_b)
    e3 = _head_expand_matrix()
    gsum = _group_sum_matrix()
    a_log_w = _pad_lanes(a_log, HEAD_LANES)
    d_skip_e = jnp.repeat(d_skip, SSM_HEAD_DIM, axis=1)
    nb, ssm_p = _ssd_prompt(xbc, dt, z, a_log_w, d_skip_e, ssm_norm_w, e3)
    nb, ssm_s = _ssd_sample(xbc, dt, z, state_ssm[0], a_log_w, d_skip_e, ssm_norm_w, e3, gsum, nb)

    k_s = cache_mem_k[0].reshape(DEC_BATCH, N_MEM * X_HEADS, X_HEAD_DIM)
    v_s = cache_mem_v[0].reshape(DEC_BATCH, N_MEM * X_HEADS, X_HEAD_DIM)
    ox = _attention(q, k_p, v_p, k_s, v_s)

    merged = _merge(u_a, nb, ox, gates, w_out_a[0], w_out_b[0], w_out_x[0])
    x1, hn = _out_proj_resid(merged, w_o[0], xp, xs, norm_mix_post, norm_mlp_pre)

    hff = _matmul(hn, w_ff1[0], n_out=D_FF, tm=1024, tn=1024, vmem_mb=48, out_dtype=BF16, epilogue="relu2",
                  name="ff1")
    ff = _matmul(hff, w_ff2[0], n_out=D_MODEL, tm=512, tn=512, vmem_mb=52, single_buffer_w=True, name="ff2")
    y_p = _resid_out(x1, ff, norm_mlp_post, row0=0, rows=T_PROMPT)
    y_s = _resid_out(x1, ff, norm_mlp_post, row0=T_PROMPT, rows=T_SAMPLE)

    return (
        y_p.reshape(BATCH, SEQ, D_MODEL),
        y_s.reshape(DEC_BATCH, DEC_SEQ, D_MODEL),
        k_p.reshape(1, BATCH, N_MEM, X_HEADS, X_HEAD_DIM),
        v_p.reshape(1, BATCH, N_MEM, X_HEADS, X_HEAD_DIM),
        conv_a_p[None],
        conv_b_p[None],
        ssm_p[None],
        conv_a_s[None],
        conv_b_s[None],
        ssm_s[None],
    )
```
